```python
import math
import jax, jax.numpy as jnp
from jax import lax
import numpy as np

D_MODEL = 4096
BATCH = 1
SEQ = 8192
DEPTH = 1
DEC_BATCH = 32
DEC_SEQ = 4
PAST_LEN = 8192
PAGE_SIZE = 128

D_RNN = D_MODEL
RNN_BLOCKS = 16
RNN_BW = D_RNN // RNN_BLOCKS
CONV_W = 4
LRU_C = 8.0
N_HEADS = 32
HEAD_DIM = D_MODEL // N_HEADS
N_KV = 8
IDX_HEADS = 16
IDX_DIM = 64
TOPK_MAX = 256
Q_BLOCK = 128
N_BUCKETS = 32
MAX_DIST = 128
N_EXPERTS = 64
TOP_K = 8
N_GROUPS = 8
TOPK_GROUPS = 4
D_EXPERT = 1024
D_SHARED = 1024
ROUTED_SCALE = 2.5
MOE_BLOCK = 128
D_PLE = 256
DN_ALPHA = (2.0 * DEPTH) ** 0.25
DN_BETA = (8.0 * DEPTH) ** -0.25
LN_EPS = 1e-5
SPLITS = (D_RNN, D_RNN, N_HEADS * HEAD_DIM, N_KV * HEAD_DIM, N_KV * HEAD_DIM,
          IDX_HEADS * IDX_DIM, IDX_DIM, IDX_HEADS, D_MODEL, D_MODEL)
D_IN = sum(SPLITS)

F32 = jnp.float32

kernel_name = "hybrid_rglru_dsa_moe_decode_step"


def _layernorm(x, g, b):
    xf = x.astype(F32)
    mu = jnp.mean(xf, axis=-1, keepdims=True)
    var = jnp.mean(jnp.square(xf - mu), axis=-1, keepdims=True)
    return ((xf - mu) * lax.rsqrt(var + LN_EPS) * g.astype(F32) + b.astype(F32)).astype(x.dtype)


def _rel_bucket(dist):
    max_exact = N_BUCKETS // 2
    d = jnp.maximum(dist, 0)
    large = max_exact + (jnp.log(jnp.maximum(d, 1).astype(F32) / max_exact)
                         / math.log(MAX_DIST / max_exact) * (N_BUCKETS - max_exact)).astype(jnp.int32)
    large = jnp.minimum(large, N_BUCKETS - 1)
    return jnp.where(d < max_exact, d, large)


def _split_in(proj):
    parts, off = [], 0
    for w in SPLITS:
        parts.append(proj[..., off:off + w])
        off += w
    return parts


def _causal_conv(x, buf, w, b):
    T = x.shape[1]
    xc = jnp.concatenate([buf.astype(x.dtype), x], axis=1)
    y = b.astype(x.dtype)
    for j in range(CONV_W):
        y = y + w[j].astype(x.dtype) * xc[:, j:j + T]
    return y, xc[:, T:]


def _rglru(xr, yr, conv_buf, h0, conv_w, conv_b, w_rg_a, b_rg_a, w_rg_i, b_rg_i, lru_lambda):
    bsz, T, _ = xr.shape
    xc, new_buf = _causal_conv(xr, conv_buf, conv_w, conv_b)
    xf = xc.astype(F32)
    xb = xf.reshape(bsz, T, RNN_BLOCKS, RNN_BW)
    r = jax.nn.sigmoid(jnp.einsum('btnc,ncd->btnd', xb, w_rg_a.astype(F32)) + b_rg_a.astype(F32)).reshape(bsz, T, D_RNN)
    i = jax.nn.sigmoid(jnp.einsum('btnc,ncd->btnd', xb, w_rg_i.astype(F32)) + b_rg_i.astype(F32)).reshape(bsz, T, D_RNN)
    log_a = -LRU_C * r * jax.nn.softplus(-lru_lambda.astype(F32))
    a = jnp.exp(log_a)
    u = jnp.sqrt(-jnp.expm1(2.0 * log_a)) * (i * xf)
    u = u.at[:, 0].add(a[:, 0] * h0.astype(F32))

    def combine(left, right):
        return left[0] * right[0], right[0] * left[1] + right[1]

    _, h = lax.associative_scan(combine, (a, u), axis=1)
    out = h * jax.nn.gelu(yr.astype(F32))
    return out.astype(xr.dtype), new_buf, h[:, -1]


def _sparse_attn(q, qi, wi, qpos, kidx, gather_kv, ktop, rel_bias):
    bsz, nq = q.shape[:2]
    L = kidx.shape[1]
    s = jnp.einsum('bqhd,bld->bqhl', qi.astype(F32), kidx) * IDX_DIM ** -0.5
    score = jnp.einsum('bqh,bqhl->bql', wi.astype(F32) * IDX_HEADS ** -0.5, jax.nn.relu(s))
    kpos = jnp.arange(L, dtype=jnp.int32)
    score = jnp.where(kpos[None, None, :] <= qpos[None, :, None], score, -jnp.inf)
    _, sel = lax.top_k(score, ktop)
    ks, vs = gather_kv(sel)
    grp = N_HEADS // N_KV
    qg = q.reshape(bsz, nq, N_KV, grp, HEAD_DIM).astype(F32)
    logits = jnp.einsum('bqkgd,bqjkd->bqkgj', qg, ks.astype(F32)) * HEAD_DIM ** -0.5
    dist = qpos[None, :, None] - sel
    bias = rel_bias.astype(F32)[_rel_bucket(dist)]
    bias = bias.reshape(bsz, nq, ktop, N_KV, grp).transpose(0, 1, 3, 4, 2)
    logits = jnp.where((dist >= 0)[:, :, None, None, :], logits + bias, -jnp.inf)
    p = jax.nn.softmax(logits, axis=-1)
    o = jnp.einsum('bqkgj,bqjkd->bqkgd', p, vs.astype(F32))
    return o.reshape(bsz, nq, N_HEADS * HEAD_DIM).astype(q.dtype)


def _take_rows(a, idx):
    return jax.vmap(lambda t, i: t[i])(a, idx)


def _prompt_attn(q, k, v, qi, ki, wi, rel_bias):
    bsz, S = q.shape[:2]
    ktop = min(TOPK_MAX, S // 4)
    kidx = ki.astype(F32)

    def gather(sel):
        return _take_rows(k, sel), _take_rows(v, sel)

    def block(b):
        q0 = b * Q_BLOCK
        sl = lambda t: lax.dynamic_slice_in_dim(t, q0, Q_BLOCK, axis=1)
        qpos = q0 + jnp.arange(Q_BLOCK, dtype=jnp.int32)
        return _sparse_attn(sl(q), sl(qi), sl(wi), qpos, kidx, gather, ktop, rel_bias)

    out = lax.map(block, jnp.arange(S // Q_BLOCK))
    return out.transpose(1, 0, 2, 3).reshape(bsz, S, N_HEADS * HEAD_DIM)


def _sample_attn(q, k, v, qi, ki, wi, cache_k, cache_v, cache_idx_k, page_table, rel_bias):
    bsz, T = q.shape[:2]
    past = page_table.shape[1] * PAGE_SIZE
    ktop = min(TOPK_MAX, (past + T) // 4)
    ki_past = cache_idx_k[page_table].reshape(bsz, past, IDX_DIM)
    kidx = jnp.concatenate([ki_past.astype(F32), ki.astype(F32)], axis=1)

    def gather(sel):
        is_new = (sel >= past)[..., None, None]
        ps = jnp.minimum(sel, past - 1)
        phys = _take_rows(page_table, ps // PAGE_SIZE)
        off = ps % PAGE_SIZE
        ns = jnp.clip(sel - past, 0, T - 1)
        kk = jnp.where(is_new, _take_rows(k, ns).astype(F32), cache_k[phys, off].astype(F32))
        vv = jnp.where(is_new, _take_rows(v, ns).astype(F32), cache_v[phys, off].astype(F32))
        return kk, vv

    qpos = past + jnp.arange(T, dtype=jnp.int32)
    return _sparse_attn(q, qi, wi, qpos, kidx, gather, ktop, rel_bias)


def _swiglu(x, w1, w3, w2):
    return (jax.nn.silu(x @ w1) * (x @ w3)) @ w2


def _moe(x, w_router, router_bias, w_e1, w_e3, w_e2, w_s1, w_s3, w_s2):
    shp = x.shape
    xt = x.reshape(-1, shp[-1])
    N = xt.shape[0]
    s = jax.nn.sigmoid(xt.astype(F32) @ w_router.astype(F32))
    sb = s + router_bias.astype(F32)
    gscore = lax.top_k(sb.reshape(N, N_GROUPS, N_EXPERTS // N_GROUPS), 2)[0].sum(-1)
    _, gidx = lax.top_k(gscore, TOPK_GROUPS)
    gmask = jax.nn.one_hot(gidx, N_GROUPS, dtype=F32).sum(1) > 0
    emask = jnp.repeat(gmask, N_EXPERTS // N_GROUPS, axis=1)
    _, eidx = lax.top_k(jnp.where(emask, sb, -jnp.inf), TOP_K)
    gw = jnp.take_along_axis(s, eidx, axis=1)
    gw = gw / jnp.sum(gw, axis=-1, keepdims=True) * ROUTED_SCALE
    A = N * TOP_K
    fe = eidx.reshape(A)
    ft = jnp.arange(A, dtype=jnp.int32) // TOP_K
    fw = gw.reshape(A)
    order = jnp.argsort(fe)
    se = fe[order]
    counts = jnp.bincount(fe, length=N_EXPERTS)
    start = jnp.cumsum(counts) - counts
    pcounts = (counts + MOE_BLOCK - 1) // MOE_BLOCK * MOE_BLOCK
    pend = jnp.cumsum(pcounts)
    pstart = pend - pcounts
    dest = pstart[se] + (jnp.arange(A, dtype=jnp.int32) - start[se])
    n_blocks = -(-A // MOE_BLOCK) + N_EXPERTS
    slot_tok = jnp.full((n_blocks * MOE_BLOCK,), N, jnp.int32).at[dest].set(ft[order])
    slot_w = jnp.zeros((n_blocks * MOE_BLOCK,), F32).at[dest].set(fw[order])
    blk_e = jnp.minimum(jnp.searchsorted(pend, jnp.arange(n_blocks, dtype=jnp.int32) * MOE_BLOCK, side='right'),
                        N_EXPERTS - 1)
    xpad = jnp.concatenate([xt, jnp.zeros((1, shp[-1]), xt.dtype)], axis=0)

    def run(args):
        tok, e = args
        return _swiglu(xpad[tok], w_e1[e], w_e3[e], w_e2[e])

    yb = lax.map(run, (slot_tok.reshape(n_blocks, MOE_BLOCK), blk_e))
    routed = jax.ops.segment_sum(yb.reshape(-1, shp[-1]).astype(F32) * slot_w[:, None], slot_tok,
                                 num_segments=N + 1)[:N]
    out = routed + _swiglu(xt, w_s1, w_s3, w_s2).astype(F32)
    return out.astype(x.dtype).reshape(shp)


def _layer(x, p_l, conv_buf, h0, attend, w_in, conv_w, conv_b, w_rg_a, b_rg_a, w_rg_i, b_rg_i,
           lru_lambda, w_rnn_out, w_att_out, w_out, ln1_g, ln1_b, w_router, router_bias,
           w_e1, w_e3, w_e2, w_s1, w_s3, w_s2, ln2_g, ln2_b, w_ple_gate, w_ple_proj):
    bsz, T, _ = x.shape
    xr, yr, q, k, v, qi, ki, wi, g_rnn, g_att = _split_in(x @ w_in)
    rnn_o, new_buf, h_last = _rglru(xr, yr, conv_buf, h0, conv_w, conv_b, w_rg_a, b_rg_a,
                                    w_rg_i, b_rg_i, lru_lambda)
    q = q.reshape(bsz, T, N_HEADS, HEAD_DIM)
    k = k.reshape(bsz, T, N_KV, HEAD_DIM)
    v = v.reshape(bsz, T, N_KV, HEAD_DIM)
    qi = qi.reshape(bsz, T, IDX_HEADS, IDX_DIM)
    att_o = attend(q, k, v, qi, ki, wi)
    merged = jax.nn.sigmoid(g_rnn) * (rnn_o @ w_rnn_out) + jax.nn.sigmoid(g_att) * (att_o @ w_att_out)
    x = _layernorm(DN_ALPHA * x + merged @ w_out, ln1_g, ln1_b)
    x = _layernorm(DN_ALPHA * x + _moe(x, w_router, router_bias, w_e1, w_e3, w_e2, w_s1, w_s3, w_s2),
                   ln2_g, ln2_b)
    x = x + jax.nn.sigmoid(x @ w_ple_gate) * (p_l.astype(x.dtype) @ w_ple_proj)
    return x, k, v, ki, new_buf, h_last


def setup_inputs(seed: int = 0) -> dict:
    key = jax.random.key(seed)
    keys = iter(jax.random.split(key, 48))

    def nrm(shape, scale):
        return jax.random.normal(next(keys), shape, F32) * scale

    n_pages = PAST_LEN // PAGE_SIZE
    n_used = DEC_BATCH * n_pages
    n_phys = n_used + max(1, n_used // 4)
    page_table = jax.random.permutation(next(keys), n_phys)[:n_used].reshape(DEC_BATCH, n_pages).astype(jnp.int32)
    u = jax.random.uniform(next(keys), (DEPTH, D_RNN), F32, 0.9, 0.999)
    a0 = u ** (1.0 / LRU_C)
    lru_lambda = jnp.log(a0) - jnp.log1p(-a0)
    dq = N_HEADS * HEAD_DIM
    return {
        "x_prompt": nrm((BATCH, SEQ, D_MODEL), 1.0),
        "x_sample": nrm((DEC_BATCH, DEC_SEQ, D_MODEL), 1.0),
        "cache_k": nrm((DEPTH, n_phys, PAGE_SIZE, N_KV, HEAD_DIM), 1.0),
        "cache_v": nrm((DEPTH, n_phys, PAGE_SIZE, N_KV, HEAD_DIM), 1.0),
        "cache_idx_k": nrm((DEPTH, n_phys, PAGE_SIZE, IDX_DIM), 1.0),
        "state_conv": nrm((DEPTH, DEC_BATCH, CONV_W - 1, D_RNN), 1.0),
        "state_rnn": nrm((DEPTH, DEC_BATCH, D_RNN), 0.5),
        "page_table": page_table,
        "p_prompt": nrm((DEPTH, BATCH, SEQ, D_PLE), 1.0),
        "p_sample": nrm((DEPTH, DEC_BATCH, DEC_SEQ, D_PLE), 1.0),
        "rel_bias": nrm((N_BUCKETS, N_HEADS), 0.5),
        "w_in": nrm((DEPTH, D_MODEL, D_IN), D_MODEL ** -0.5),
        "conv_w": nrm((DEPTH, CONV_W, D_RNN), CONV_W ** -0.5),
        "conv_b": nrm((DEPTH, D_RNN), 0.01),
        "w_rg_a": nrm((DEPTH, RNN_BLOCKS, RNN_BW, RNN_BW), RNN_BW ** -0.5),
        "b_rg_a": nrm((DEPTH, RNN_BLOCKS, RNN_BW), 0.01),
        "w_rg_i": nrm((DEPTH, RNN_BLOCKS, RNN_BW, RNN_BW), RNN_BW ** -0.5),
        "b_rg_i": nrm((DEPTH, RNN_BLOCKS, RNN_BW), 0.01),
        "lru_lambda": lru_lambda,
        "w_rnn_out": nrm((DEPTH, D_RNN, D_MODEL), DN_BETA * D_RNN ** -0.5),
        "w_att_out": nrm((DEPTH, dq, D_MODEL), DN_BETA * dq ** -0.5),
        "w_out": nrm((DEPTH, D_MODEL, D_MODEL), DN_BETA * D_MODEL ** -0.5),
        "ln1_g": 1.0 + nrm((DEPTH, D_MODEL), 0.01),
        "ln1_b": nrm((DEPTH, D_MODEL), 0.01),
        "w_router": nrm((DEPTH, D_MODEL, N_EXPERTS), D_MODEL ** -0.5),
        "router_bias": nrm((DEPTH, N_EXPERTS), 0.01),
        "w_e1": nrm((DEPTH, N_EXPERTS, D_MODEL, D_EXPERT), D_MODEL ** -0.5),
        "w_e3": nrm((DEPTH, N_EXPERTS, D_MODEL, D_EXPERT), D_MODEL ** -0.5),
        "w_e2": nrm((DEPTH, N_EXPERTS, D_EXPERT, D_MODEL), DN_BETA * D_EXPERT ** -0.5),
        "w_s1": nrm((DEPTH, D_MODEL, D_SHARED), D_MODEL ** -0.5),
        "w_s3": nrm((DEPTH, D_MODEL, D_SHARED), D_MODEL ** -0.5),
        "w_s2": nrm((DEPTH, D_SHARED, D_MODEL), DN_BETA * D_SHARED ** -0.5),
        "ln2_g": 1.0 + nrm((DEPTH, D_MODEL), 0.01),
        "ln2_b": nrm((DEPTH, D_MODEL), 0.01),
        "w_ple_gate": nrm((DEPTH, D_MODEL, D_MODEL), D_MODEL ** -0.5),
        "w_ple_proj": nrm((DEPTH, D_PLE, D_MODEL), D_PLE ** -0.5),
    }


def reference(x_prompt, x_sample, cache_k, cache_v, cache_idx_k, state_conv, state_rnn, page_table,
              p_prompt, p_sample, rel_bias, w_in, conv_w, conv_b, w_rg_a, b_rg_a, w_rg_i, b_rg_i,
              lru_lambda, w_rnn_out, w_att_out, w_out, ln1_g, ln1_b, w_router, router_bias,
              w_e1, w_e3, w_e2, w_s1, w_s3, w_s2, ln2_g, ln2_b, w_ple_gate, w_ple_proj):
    yp, ys = x_prompt, x_sample
    new_p, new_s = [], []
    for l in range(DEPTH):
        lw = (w_in[l], conv_w[l], conv_b[l], w_rg_a[l], b_rg_a[l], w_rg_i[l], b_rg_i[l],
              lru_lambda[l], w_rnn_out[l], w_att_out[l], w_out[l], ln1_g[l], ln1_b[l],
              w_router[l], router_bias[l], w_e1[l], w_e3[l], w_e2[l], w_s1[l], w_s3[l], w_s2[l],
              ln2_g[l], ln2_b[l], w_ple_gate[l], w_ple_proj[l])
        ck, cv, cik = cache_k[l], cache_v[l], cache_idx_k[l]
        attend_p = lambda q, k, v, qi, ki, wi: _prompt_attn(q, k, v, qi, ki, wi, rel_bias)
        attend_s = lambda q, k, v, qi, ki, wi: _sample_attn(q, k, v, qi, ki, wi, ck, cv, cik,
                                                            page_table, rel_bias)
        buf0 = jnp.zeros((yp.shape[0], CONV_W - 1, D_RNN), yp.dtype)
        h00 = jnp.zeros((yp.shape[0], D_RNN), F32)
        yp, kp, vp, kip, bp, hp = _layer(yp, p_prompt[l], buf0, h00, attend_p, *lw)
        ys, ks_, vs_, kis, bs, hs = _layer(ys, p_sample[l], state_conv[l], state_rnn[l], attend_s, *lw)
        new_p.append((kp, vp, kip, bp, hp))
        new_s.append((ks_, vs_, kis, bs, hs))
    k_prompt = jnp.stack([t[0] for t in new_p])
    v_prompt = jnp.stack([t[1] for t in new_p])
    idx_k_prompt = jnp.stack([t[2] for t in new_p])
    conv_prompt = jnp.stack([t[3] for t in new_p])
    rnn_prompt = jnp.stack([t[4] for t in new_p])
    k_sample = jnp.stack([t[0] for t in new_s])
    v_sample = jnp.stack([t[1] for t in new_s])
    idx_k_sample = jnp.stack([t[2] for t in new_s])
    conv_sample = jnp.stack([t[3] for t in new_s])
    rnn_sample = jnp.stack([t[4] for t in new_s])
    return (yp, ys, k_prompt, v_prompt, idx_k_prompt, conv_prompt, rnn_prompt,
            k_sample, v_sample, idx_k_sample, conv_sample, rnn_sample)
```

```python
import functools
import math

import jax
import jax.numpy as jnp
from jax import lax
from jax.experimental import pallas as pl
from jax.experimental.pallas import tpu as pltpu

D_MODEL = 4096
SEQ = 8192
DEC_BATCH = 32
DEC_SEQ = 4
PAGE_SIZE = 128
D_RNN = D_MODEL
RNN_BLOCKS = 16
RNN_BW = D_RNN // RNN_BLOCKS
CONV_W = 4
LRU_C = 8.0
N_HEADS = 32
HEAD_DIM = D_MODEL // N_HEADS
N_KV = 8
IDX_HEADS = 16
IDX_DIM = 64
TOPK_MAX = 256
Q_BLOCK = 128
N_BUCKETS = 32
MAX_DIST = 128
N_EXPERTS = 64
TOP_K = 8
N_GROUPS = 8
TOPK_GROUPS = 4
D_EXPERT = 1024
ROUTED_SCALE = 2.5
DN_ALPHA = 2.0 ** 0.25
LN_EPS = 1e-5
SPLITS = (D_RNN, D_RNN, N_HEADS * HEAD_DIM, N_KV * HEAD_DIM, N_KV * HEAD_DIM,
          IDX_HEADS * IDX_DIM, IDX_DIM, IDX_HEADS, D_MODEL, D_MODEL)

F32 = jnp.float32
BF16 = jnp.bfloat16

N_TOK = SEQ + DEC_BATCH * DEC_SEQ
MM_TM = 640
MM_TN = 512
MOE_MB = 256
MOE_NB = -(-(N_TOK * TOP_K) // MOE_MB) + N_EXPERTS
MOE_CE = 256
VMEM_LIMIT = 48 * 1024 * 1024


def _mm_kernel(x_ref, w_ref, o_ref, wbf_ref):
    @pl.when(pl.program_id(1) == 0)
    def _():
        wbf_ref[...] = w_ref[...].astype(BF16)

    o_ref[...] = jnp.dot(x_ref[...], wbf_ref[...], preferred_element_type=F32).astype(o_ref.dtype)


def _mm(x, w, col0=0, ncols=None, out_dtype=F32, tm=MM_TM, tn=MM_TN):
    m, k = x.shape
    ncols = w.shape[1] - col0 if ncols is None else ncols
    tn = min(tn, ncols)
    assert m % tm == 0 and ncols % tn == 0 and col0 % tn == 0
    cb = col0 // tn
    return pl.pallas_call(
        _mm_kernel,
        grid=(ncols // tn, m // tm),
        in_specs=[pl.BlockSpec((tm, k), lambda j, i: (i, 0)),
                  pl.BlockSpec((k, tn), lambda j, i: (0, j + cb))],
        out_specs=pl.BlockSpec((tm, tn), lambda j, i: (i, j)),
        out_shape=jax.ShapeDtypeStruct((m, ncols), out_dtype),
        scratch_shapes=[pltpu.VMEM((k, tn), BF16)],
        compiler_params=pltpu.CompilerParams(dimension_semantics=("arbitrary", "arbitrary"),
                                             vmem_limit_bytes=VMEM_LIMIT),
    )(x, w)


def _mm_f32_kernel(x_ref, w_ref, o_ref):
    o_ref[...] = jnp.dot(x_ref[...], w_ref[...], preferred_element_type=F32,
                         precision=lax.Precision.HIGHEST)


def _mm_f32(x, w, tm=MM_TM):
    m, k = x.shape
    n = w.shape[1]
    return pl.pallas_call(
        _mm_f32_kernel,
        grid=(m // tm,),
        in_specs=[pl.BlockSpec((tm, k), lambda i: (i, 0)),
                  pl.BlockSpec((k, n), lambda i: (0, 0))],
        out_specs=pl.BlockSpec((tm, n), lambda i: (i, 0)),
        out_shape=jax.ShapeDtypeStruct((m, n), F32),
        compiler_params=pltpu.CompilerParams(dimension_semantics=("arbitrary",),
                                             vmem_limit_bytes=VMEM_LIMIT),
    )(x, w)


def _expert_up_kernel(be_ref, x_ref, w1_ref, w3_ref, h_ref, w1bf_ref, w3bf_ref):
    b = pl.program_id(1)
    prev = be_ref[jnp.maximum(b - 1, 0)]

    @pl.when((b == 0) | (be_ref[b] != prev))
    def _():
        w1bf_ref[...] = w1_ref[0].astype(BF16)
        w3bf_ref[...] = w3_ref[0].astype(BF16)

    x = x_ref[...]
    a = jnp.dot(x, w1bf_ref[...], preferred_element_type=F32)
    g = jnp.dot(x, w3bf_ref[...], preferred_element_type=F32)
    h_ref[...] = (a * jax.nn.sigmoid(a) * g).astype(h_ref.dtype)


def _expert_down_kernel(be_ref, h_ref, w2_ref, y_ref, w2bf_ref):
    b = pl.program_id(1)
    prev = be_ref[jnp.maximum(b - 1, 0)]

    @pl.when((b == 0) | (be_ref[b] != prev))
    def _():
        w2bf_ref[...] = w2_ref[0].astype(BF16)

    y_ref[...] = jnp.dot(h_ref[...], w2bf_ref[...], preferred_element_type=F32)


def _experts(xs, blk_e, w_e1, w_e3, w_e2):
    d = xs.shape[1]
    nce = D_EXPERT // MOE_CE
    h = pl.pallas_call(
        _expert_up_kernel,
        grid_spec=pltpu.PrefetchScalarGridSpec(
            num_scalar_prefetch=1,
            grid=(nce, MOE_NB),
            in_specs=[pl.BlockSpec((MOE_MB, d), lambda c, b, be: (b, 0)),
                      pl.BlockSpec((1, d, MOE_CE), lambda c, b, be: (be[b], 0, c)),
                      pl.BlockSpec((1, d, MOE_CE), lambda c, b, be: (be[b], 0, c))],
            out_specs=pl.BlockSpec((MOE_MB, MOE_CE), lambda c, b, be: (b, c)),
            scratch_shapes=[pltpu.VMEM((d, MOE_CE), BF16), pltpu.VMEM((d, MOE_CE), BF16)]),
        out_shape=jax.ShapeDtypeStruct((MOE_NB * MOE_MB, D_EXPERT), BF16),
        compiler_params=pltpu.CompilerParams(dimension_semantics=("arbitrary", "arbitrary"),
                                             vmem_limit_bytes=VMEM_LIMIT),
    )(blk_e, xs, w_e1, w_e3)
    tn = 1024
    return pl.pallas_call(
        _expert_down_kernel,
        grid_spec=pltpu.PrefetchScalarGridSpec(
            num_scalar_prefetch=1,
            grid=(d // tn, MOE_NB),
            in_specs=[pl.BlockSpec((MOE_MB, D_EXPERT), lambda c, b, be: (b, 0)),
                      pl.BlockSpec((1, D_EXPERT, tn), lambda c, b, be: (be[b], 0, c))],
            out_specs=pl.BlockSpec((MOE_MB, tn), lambda c, b, be: (b, c)),
            scratch_shapes=[pltpu.VMEM((D_EXPERT, tn), BF16)]),
        out_shape=jax.ShapeDtypeStruct((MOE_NB * MOE_MB, d), F32),
        compiler_params=pltpu.CompilerParams(dimension_semantics=("arbitrary", "arbitrary"),
                                             vmem_limit_bytes=VMEM_LIMIT),
    )(blk_e, h, w_e2)


def _layernorm(x, g, b):
    mu = jnp.mean(x, axis=-1, keepdims=True)
    var = jnp.mean(jnp.square(x - mu), axis=-1, keepdims=True)
    return (x - mu) * lax.rsqrt(var + LN_EPS) * g + b


def _rel_bucket(dist):
    max_exact = N_BUCKETS // 2
    d = jnp.maximum(dist, 0)
    large = max_exact + (jnp.log(jnp.maximum(d, 1).astype(F32) / max_exact)
                         / math.log(MAX_DIST / max_exact) * (N_BUCKETS - max_exact)).astype(jnp.int32)
    large = jnp.minimum(large, N_BUCKETS - 1)
    return jnp.where(d < max_exact, d, large)


def _causal_conv(x, buf, w, b):
    T = x.shape[1]
    xc = jnp.concatenate([buf, x], axis=1)
    y = b
    for j in range(CONV_W):
        y = y + w[j] * xc[:, j:j + T]
    return y, xc[:, T:]


def _rglru(xr, yr, conv_buf, h0, conv_w, conv_b, w_rg_a, b_rg_a, w_rg_i, b_rg_i, lru_lambda):
    bsz, T, _ = xr.shape
    xf, new_buf = _causal_conv(xr, conv_buf, conv_w, conv_b)
    xb = xf.reshape(bsz, T, RNN_BLOCKS, RNN_BW)
    r = jax.nn.sigmoid(jnp.einsum('btnc,ncd->btnd', xb, w_rg_a) + b_rg_a).reshape(bsz, T, D_RNN)
    i = jax.nn.sigmoid(jnp.einsum('btnc,ncd->btnd', xb, w_rg_i) + b_rg_i).reshape(bsz, T, D_RNN)
    log_a = -LRU_C * r * jax.nn.softplus(-lru_lambda)
    a = jnp.exp(log_a)
    u = jnp.sqrt(-jnp.expm1(2.0 * log_a)) * (i * xf)
    u = u.at[:, 0].add(a[:, 0] * h0)

    def combine(left, right):
        return left[0] * right[0], right[0] * left[1] + right[1]

    _, h = lax.associative_scan(combine, (a, u), axis=1)
    return h * jax.nn.gelu(yr), new_buf, h[:, -1]


def _sparse_attn(q, qi, wi, qpos, kidx, gather_kv, ktop, rel_bias):
    bsz, nq = q.shape[:2]
    L = kidx.shape[1]
    s = jnp.einsum('bqhd,bld->bqhl', qi, kidx) * IDX_DIM ** -0.5
    score = jnp.einsum('bqh,bqhl->bql', wi * IDX_HEADS ** -0.5, jax.nn.relu(s))
    kpos = jnp.arange(L, dtype=jnp.int32)
    score = jnp.where(kpos[None, None, :] <= qpos[None, :, None], score, -jnp.inf)
    _, sel = lax.top_k(score, ktop)
    ks, vs = gather_kv(sel)
    grp = N_HEADS // N_KV
    qg = q.reshape(bsz, nq, N_KV, grp, HEAD_DIM)
    logits = jnp.einsum('bqkgd,bqjkd->bqkgj', qg, ks) * HEAD_DIM ** -0.5
    dist = qpos[None, :, None] - sel
    bias = rel_bias[_rel_bucket(dist)]
    bias = bias.reshape(bsz, nq, ktop, N_KV, grp).transpose(0, 1, 3, 4, 2)
    logits = jnp.where((dist >= 0)[:, :, None, None, :], logits + bias, -jnp.inf)
    p = jax.nn.softmax(logits, axis=-1)
    o = jnp.einsum('bqkgj,bqjkd->bqkgd', p, vs)
    return o.reshape(bsz, nq, N_HEADS * HEAD_DIM)


def _take_rows(a, idx):
    return jax.vmap(lambda t, i: t[i])(a, idx)


def _prompt_attn(q, k, v, qi, ki, wi, rel_bias):
    bsz, S = q.shape[:2]
    ktop = min(TOPK_MAX, S // 4)

    def gather(sel):
        return _take_rows(k, sel), _take_rows(v, sel)

    def block(b):
        q0 = b * Q_BLOCK
        sl = lambda t: lax.dynamic_slice_in_dim(t, q0, Q_BLOCK, axis=1)
        qpos = q0 + jnp.arange(Q_BLOCK, dtype=jnp.int32)
        return _sparse_attn(sl(q), sl(qi), sl(wi), qpos, ki, gather, ktop, rel_bias)

    out = lax.map(block, jnp.arange(S // Q_BLOCK))
    return out.transpose(1, 0, 2, 3).reshape(bsz, S, N_HEADS * HEAD_DIM)


def _sample_attn(q, k, v, qi, ki, wi, cache_k, cache_v, cache_idx_k, page_table, rel_bias):
    bsz, T = q.shape[:2]
    past = page_table.shape[1] * PAGE_SIZE
    ktop = min(TOPK_MAX, (past + T) // 4)
    ki_past = cache_idx_k[page_table].reshape(bsz, past, IDX_DIM)
    kidx = jnp.concatenate([ki_past, ki], axis=1)

    def gather(sel):
        is_new = (sel >= past)[..., None, None]
        ps = jnp.minimum(sel, past - 1)
        phys = _take_rows(page_table, ps // PAGE_SIZE)
        off = ps % PAGE_SIZE
        ns = jnp.clip(sel - past, 0, T - 1)
        kk = jnp.where(is_new, _take_rows(k, ns), cache_k[phys, off])
        vv = jnp.where(is_new, _take_rows(v, ns), cache_v[phys, off])
        return kk, vv

    qpos = past + jnp.arange(T, dtype=jnp.int32)
    return _sparse_attn(q, qi, wi, qpos, kidx, gather, ktop, rel_bias)


def _route(logits, router_bias):
    n = logits.shape[0]
    s = jax.nn.sigmoid(logits)
    sb = s + router_bias
    gscore = lax.top_k(sb.reshape(n, N_GROUPS, N_EXPERTS // N_GROUPS), 2)[0].sum(-1)
    _, gidx = lax.top_k(gscore, TOPK_GROUPS)
    gmask = jax.nn.one_hot(gidx, N_GROUPS, dtype=F32).sum(1) > 0
    emask = jnp.repeat(gmask, N_EXPERTS // N_GROUPS, axis=1)
    _, eidx = lax.top_k(jnp.where(emask, sb, -jnp.inf), TOP_K)
    gw = jnp.take_along_axis(s, eidx, axis=1)
    gw = gw / jnp.sum(gw, axis=-1, keepdims=True) * ROUTED_SCALE
    a = n * TOP_K
    fe = eidx.reshape(a)
    order = jnp.argsort(fe)
    se = fe[order]
    counts = jnp.bincount(fe, length=N_EXPERTS)
    start = jnp.cumsum(counts) - counts
    pcounts = (counts + MOE_MB - 1) // MOE_MB * MOE_MB
    pend = jnp.cumsum(pcounts)
    pstart = pend - pcounts
    dest_sorted = pstart[se] + (jnp.arange(a, dtype=jnp.int32) - start[se])
    dest = jnp.zeros((a,), jnp.int32).at[order].set(dest_sorted.astype(jnp.int32))
    slot_tok = jnp.full((MOE_NB * MOE_MB,), n, jnp.int32).at[dest].set(
        jnp.arange(a, dtype=jnp.int32) // TOP_K)
    blk_e = jnp.minimum(
        jnp.searchsorted(pend, jnp.arange(MOE_NB, dtype=jnp.int32) * MOE_MB, side='right'),
        N_EXPERTS - 1).astype(jnp.int32)
    return slot_tok, dest.reshape(n, TOP_K), gw, blk_e


def kernel(x_prompt, x_sample, cache_k, cache_v, cache_idx_k, state_conv, state_rnn, page_table,
           p_prompt, p_sample, rel_bias, w_in, conv_w, conv_b, w_rg_a, b_rg_a, w_rg_i, b_rg_i,
           lru_lambda, w_rnn_out, w_att_out, w_out, ln1_g, ln1_b, w_router, router_bias,
           w_e1, w_e3, w_e2, w_s1, w_s3, w_s2, ln2_g, ln2_b, w_ple_gate, w_ple_proj):
    ns = DEC_BATCH * DEC_SEQ
    x_all = jnp.concatenate([x_prompt[0], x_sample.reshape(ns, D_MODEL)], axis=0)
    xb = x_all.astype(BF16)
    win = w_in[0]
    offs = [0]
    for w in SPLITS:
        offs.append(offs[-1] + w)
    xr, yr, q, k, v, qi = [_mm(xb, win, offs[g], SPLITS[g]) for g in range(6)]
    w_small = jnp.pad(win[:, offs[6]:offs[8]], ((0, 0), (0, 128 - IDX_DIM - IDX_HEADS)))
    kw = _mm(xb, w_small, tn=128)
    ki, wi = kw[:, :IDX_DIM], kw[:, IDX_DIM:IDX_DIM + IDX_HEADS]
    w_gate = win[:, offs[8]:]
    g_rnn = _mm(xb, w_gate, 0, D_MODEL)
    g_att = _mm(xb, w_gate, D_MODEL, D_MODEL)

    sp = lambda t: (t[:SEQ][None], t[SEQ:].reshape(DEC_BATCH, DEC_SEQ, -1))
    lw = (conv_w[0], conv_b[0], w_rg_a[0], b_rg_a[0], w_rg_i[0], b_rg_i[0], lru_lambda[0])
    xr_p, xr_s = sp(xr)
    yr_p, yr_s = sp(yr)
    rnn_p, buf_p, h_p = _rglru(xr_p, yr_p, jnp.zeros((1, CONV_W - 1, D_RNN), F32),
                               jnp.zeros((1, D_RNN), F32), *lw)
    rnn_s, buf_s, h_s = _rglru(xr_s, yr_s, state_conv[0], state_rnn[0], *lw)

    q_p, q_s = sp(q)
    k_p, k_s = sp(k)
    v_p, v_s = sp(v)
    qi_p, qi_s = sp(qi)
    ki_p, ki_s = sp(ki)
    wi_p, wi_s = sp(wi)
    hd = lambda t, n, d: t.reshape(t.shape[0], t.shape[1], n, d)
    k_p, k_s = hd(k_p, N_KV, HEAD_DIM), hd(k_s, N_KV, HEAD_DIM)
    v_p, v_s = hd(v_p, N_KV, HEAD_DIM), hd(v_s, N_KV, HEAD_DIM)
    att_p = _prompt_attn(hd(q_p, N_HEADS, HEAD_DIM), k_p, v_p, hd(qi_p, IDX_HEADS, IDX_DIM),
                         ki_p, wi_p, rel_bias)
    att_s = _sample_attn(hd(q_s, N_HEADS, HEAD_DIM), k_s, v_s, hd(qi_s, IDX_HEADS, IDX_DIM),
                         ki_s, wi_s, cache_k[0], cache_v[0], cache_idx_k[0], page_table, rel_bias)

    rnn_o = jnp.concatenate([rnn_p[0], rnn_s.reshape(ns, D_RNN)], axis=0).astype(BF16)
    att_o = jnp.concatenate([att_p[0], att_s.reshape(ns, D_MODEL)], axis=0).astype(BF16)
    merged = (jax.nn.sigmoid(g_rnn) * _mm(rnn_o, w_rnn_out[0])
              + jax.nn.sigmoid(g_att) * _mm(att_o, w_att_out[0]))
    x1 = _layernorm(DN_ALPHA * x_all + _mm(merged.astype(BF16), w_out[0]), ln1_g[0], ln1_b[0])

    w_router_p = jnp.pad(w_router[0], ((0, 0), (0, 128 - N_EXPERTS)))
    logits = _mm_f32(x1, w_router_p)[:, :N_EXPERTS]
    slot_tok, dest, gw, blk_e = _route(logits, router_bias[0])
    x1b = x1.astype(BF16)
    xpad = jnp.concatenate([x1b, jnp.zeros((1, D_MODEL), BF16)], axis=0)
    yb = _experts(xpad[slot_tok], blk_e, w_e1[0], w_e3[0], w_e2[0])
    routed = jnp.sum(yb[dest] * gw[:, :, None], axis=1)
    hs = _mm(x1b, w_s1[0])
    hs = (hs * jax.nn.sigmoid(hs) * _mm(x1b, w_s3[0])).astype(BF16)
    moe = routed + _mm(hs, w_s2[0])
    x2 = _layernorm(DN_ALPHA * x1 + moe, ln2_g[0], ln2_b[0])

    p_all = jnp.concatenate([p_prompt[0, 0], p_sample[0].reshape(ns, -1)], axis=0).astype(BF16)
    y = x2 + jax.nn.sigmoid(_mm(x2.astype(BF16), w_ple_gate[0])) * _mm(p_all, w_ple_proj[0])

    return (y[:SEQ][None], y[SEQ:].reshape(DEC_BATCH, DEC_SEQ, D_MODEL),
            k_p[None], v_p[None], ki_p[None], buf_p[None], h_p[None],
            k_s[None], v_s[None], ki_s[None], buf_s[None], h_s[None])
```

```python
import functools
import math

import jax
import jax.numpy as jnp
from jax import lax
from jax.experimental import pallas as pl
from jax.experimental.pallas import tpu as pltpu

D_MODEL = 4096
SEQ = 8192
DEC_BATCH = 32
DEC_SEQ = 4
PAGE_SIZE = 128
D_RNN = D_MODEL
RNN_BLOCKS = 16
RNN_BW = D_RNN // RNN_BLOCKS
CONV_W = 4
LRU_C = 8.0
N_HEADS = 32
HEAD_DIM = D_MODEL // N_HEADS
N_KV = 8
IDX_HEADS = 16
IDX_DIM = 64
TOPK_MAX = 256
Q_BLOCK = 128
N_BUCKETS = 32
MAX_DIST = 128
N_EXPERTS = 64
TOP_K = 8
N_GROUPS = 8
TOPK_GROUPS = 4
D_EXPERT = 1024
ROUTED_SCALE = 2.5
DN_ALPHA = 2.0 ** 0.25
LN_EPS = 1e-5
SPLITS = (D_RNN, D_RNN, N_HEADS * HEAD_DIM, N_KV * HEAD_DIM, N_KV * HEAD_DIM,
          IDX_HEADS * IDX_DIM, IDX_DIM, IDX_HEADS, D_MODEL, D_MODEL)

F32 = jnp.float32
BF16 = jnp.bfloat16

N_TOK = SEQ + DEC_BATCH * DEC_SEQ
MM_TM = 640
MM_TN = 512
MOE_MB = 256
MOE_NB = -(-(N_TOK * TOP_K) // MOE_MB) + N_EXPERTS
MOE_CE = 256
VMEM_LIMIT = 48 * 1024 * 1024


def _mm_kernel(x_ref, w_ref, o_ref, wbf_ref):
    @pl.when(pl.program_id(1) == 0)
    def _():
        wbf_ref[...] = w_ref[...].astype(BF16)

    o_ref[...] = jnp.dot(x_ref[...], wbf_ref[...], preferred_element_type=F32).astype(o_ref.dtype)


def _mm(x, w, col0=0, ncols=None, out_dtype=F32, tm=MM_TM, tn=MM_TN):
    m, k = x.shape
    ncols = w.shape[1] - col0 if ncols is None else ncols
    tn = min(tn, ncols)
    assert m % tm == 0 and ncols % tn == 0 and col0 % tn == 0
    cb = col0 // tn
    return pl.pallas_call(
        _mm_kernel,
        grid=(ncols // tn, m // tm),
        in_specs=[pl.BlockSpec((tm, k), lambda j, i: (i, 0)),
                  pl.BlockSpec((k, tn), lambda j, i: (0, j + cb))],
        out_specs=pl.BlockSpec((tm, tn), lambda j, i: (i, j)),
        out_shape=jax.ShapeDtypeStruct((m, ncols), out_dtype),
        scratch_shapes=[pltpu.VMEM((k, tn), BF16)],
        compiler_params=pltpu.CompilerParams(dimension_semantics=("arbitrary", "arbitrary"),
                                             vmem_limit_bytes=VMEM_LIMIT),
    )(x, w)


def _mm_f32_kernel(x_ref, w_ref, o_ref):
    o_ref[...] = jnp.dot(x_ref[...], w_ref[...], preferred_element_type=F32,
                         precision=lax.Precision.HIGHEST)


def _mm_f32(x, w, tm=MM_TM):
    m, k = x.shape
    n = w.shape[1]
    return pl.pallas_call(
        _mm_f32_kernel,
        grid=(m // tm,),
        in_specs=[pl.BlockSpec((tm, k), lambda i: (i, 0)),
                  pl.BlockSpec((k, n), lambda i: (0, 0))],
        out_specs=pl.BlockSpec((tm, n), lambda i: (i, 0)),
        out_shape=jax.ShapeDtypeStruct((m, n), F32),
        compiler_params=pltpu.CompilerParams(dimension_semantics=("arbitrary",),
                                             vmem_limit_bytes=VMEM_LIMIT),
    )(x, w)


def _expert_up_kernel(be_ref, x_ref, w1_ref, w3_ref, h_ref, w1bf_ref, w3bf_ref):
    b = pl.program_id(1)
    prev = be_ref[jnp.maximum(b - 1, 0)]

    @pl.when((b == 0) | (be_ref[b] != prev))
    def _():
        w1bf_ref[...] = w1_ref[0].astype(BF16)
        w3bf_ref[...] = w3_ref[0].astype(BF16)

    x = x_ref[...]
    a = jnp.dot(x, w1bf_ref[...], preferred_element_type=F32)
    g = jnp.dot(x, w3bf_ref[...], preferred_element_type=F32)
    h_ref[...] = (a * jax.nn.sigmoid(a) * g).astype(h_ref.dtype)


def _expert_down_kernel(be_ref, h_ref, w2_ref, y_ref, w2bf_ref):
    b = pl.program_id(1)
    prev = be_ref[jnp.maximum(b - 1, 0)]

    @pl.when((b == 0) | (be_ref[b] != prev))
    def _():
        w2bf_ref[...] = w2_ref[0].astype(BF16)

    y_ref[...] = jnp.dot(h_ref[...], w2bf_ref[...], preferred_element_type=F32)


def _experts(xs, blk_e, w_e1, w_e3, w_e2):
    d = xs.shape[1]
    nce = D_EXPERT // MOE_CE
    h = pl.pallas_call(
        _expert_up_kernel,
        grid_spec=pltpu.PrefetchScalarGridSpec(
            num_scalar_prefetch=1,
            grid=(nce, MOE_NB),
            in_specs=[pl.BlockSpec((MOE_MB, d), lambda c, b, be: (b, 0)),
                      pl.BlockSpec((1, d, MOE_CE), lambda c, b, be: (be[b], 0, c)),
                      pl.BlockSpec((1, d, MOE_CE), lambda c, b, be: (be[b], 0, c))],
            out_specs=pl.BlockSpec((MOE_MB, MOE_CE), lambda c, b, be: (b, c)),
            scratch_shapes=[pltpu.VMEM((d, MOE_CE), BF16), pltpu.VMEM((d, MOE_CE), BF16)]),
        out_shape=jax.ShapeDtypeStruct((MOE_NB * MOE_MB, D_EXPERT), BF16),
        compiler_params=pltpu.CompilerParams(dimension_semantics=("arbitrary", "arbitrary"),
                                             vmem_limit_bytes=VMEM_LIMIT),
    )(blk_e, xs, w_e1, w_e3)
    tn = 1024
    return pl.pallas_call(
        _expert_down_kernel,
        grid_spec=pltpu.PrefetchScalarGridSpec(
            num_scalar_prefetch=1,
            grid=(d // tn, MOE_NB),
            in_specs=[pl.BlockSpec((MOE_MB, D_EXPERT), lambda c, b, be: (b, 0)),
                      pl.BlockSpec((1, D_EXPERT, tn), lambda c, b, be: (be[b], 0, c))],
            out_specs=pl.BlockSpec((MOE_MB, tn), lambda c, b, be: (b, c)),
            scratch_shapes=[pltpu.VMEM((D_EXPERT, tn), BF16)]),
        out_shape=jax.ShapeDtypeStruct((MOE_NB * MOE_MB, d), F32),
        compiler_params=pltpu.CompilerParams(dimension_semantics=("arbitrary", "arbitrary"),
                                             vmem_limit_bytes=VMEM_LIMIT),
    )(blk_e, h, w_e2)


IDX_QB = 128
ATT_TK = 256
ATT_QB = 256
NEG = -1e30
I32_MIN = -2 ** 31
_NT = (((1,), (1,)), ((), ()))


def _index_select_kernel(qi_ref, wi_ref, ki_ref, mask_ref, skey_ref, wb_ref, qh_ref):
    b = pl.program_id(0)
    n_tiles = mask_ref.shape[1]
    nt = (b * IDX_QB + IDX_QB + ATT_TK - 1) // ATT_TK
    qi = (qi_ref[...] * IDX_DIM ** -0.5).astype(BF16)
    w = wi_ref[...] * IDX_HEADS ** -0.5
    for h in range(IDX_HEADS):
        qh_ref[h] = qi[:, h * IDX_DIM:(h + 1) * IDX_DIM]
        wb_ref[h] = jnp.broadcast_to(w[:, h:h + 1], (IDX_QB, ATT_TK))
    row = b * IDX_QB + lax.broadcasted_iota(jnp.int32, (IDX_QB, ATT_TK), 0)
    col = lax.broadcasted_iota(jnp.int32, (IDX_QB, ATT_TK), 1)

    def score_tile(j, carry):
        kt = ki_ref[pl.ds(pl.multiple_of(j * ATT_TK, ATT_TK), ATT_TK), :]
        acc = jnp.zeros((IDX_QB, ATT_TK), F32)
        for h in range(IDX_HEADS):
            s = lax.dot_general(qh_ref[h], kt, _NT, preferred_element_type=F32)
            acc = acc + wb_ref[h] * jnp.maximum(s, 0.0)
        bits = pltpu.bitcast(acc, jnp.int32)
        key = bits ^ ((bits >> 31) & 0x7FFFFFFF)
        skey_ref[j] = jnp.where(col + j * ATT_TK <= row, key, I32_MIN)
        return carry

    lax.fori_loop(0, nt, score_tile, 0)

    half = ATT_TK // 2

    def bit_body(i, ans):
        cand_u = ans | lax.shift_left(jnp.int32(1), 31 - i)
        cand = jnp.broadcast_to(cand_u ^ I32_MIN, (IDX_QB, half))

        def count(j, c):
            t = skey_ref[j]
            return (c + jnp.where(t[:, :half] >= cand, 1, 0) + jnp.where(t[:, half:] >= cand, 1, 0))

        c = lax.fori_loop(0, nt, count, jnp.zeros((IDX_QB, half), jnp.int32))
        cnt = jnp.sum(c, axis=1, keepdims=True)
        return jnp.where(cnt >= TOPK_MAX, cand_u, ans)

    ans = lax.fori_loop(0, 32, bit_body, jnp.zeros((IDX_QB, 1), jnp.int32))
    thr = jnp.broadcast_to(jnp.maximum(ans ^ I32_MIN, I32_MIN + 1), (IDX_QB, ATT_TK))

    def write_tile(j, carry):
        mask_ref[0, j] = jnp.where(skey_ref[j] >= thr, 0.0, NEG).astype(BF16)
        return carry

    def write_empty(j, carry):
        mask_ref[0, j] = jnp.full((IDX_QB, ATT_TK), NEG, BF16)
        return carry

    lax.fori_loop(0, nt, write_tile, 0)
    lax.fori_loop(nt, n_tiles, write_empty, 0)


def _index_select(qi, ki, wi):
    s = qi.shape[0]
    n_tiles = s // ATT_TK
    return pl.pallas_call(
        _index_select_kernel,
        grid=(s // IDX_QB,),
        in_specs=[pl.BlockSpec((IDX_QB, IDX_HEADS * IDX_DIM), lambda b: (b, 0)),
                  pl.BlockSpec((IDX_QB, IDX_HEADS), lambda b: (b, 0)),
                  pl.BlockSpec((s, IDX_DIM), lambda b: (0, 0))],
        out_specs=pl.BlockSpec((1, n_tiles, IDX_QB, ATT_TK), lambda b: (b, 0, 0, 0)),
        out_shape=jax.ShapeDtypeStruct((s // IDX_QB, n_tiles, IDX_QB, ATT_TK), BF16),
        scratch_shapes=[pltpu.VMEM((n_tiles, IDX_QB, ATT_TK), jnp.int32),
                        pltpu.VMEM((IDX_HEADS, IDX_QB, ATT_TK), F32),
                        pltpu.VMEM((IDX_HEADS, IDX_QB, IDX_DIM), BF16)],
        compiler_params=pltpu.CompilerParams(dimension_semantics=("arbitrary",),
                                             vmem_limit_bytes=VMEM_LIMIT),
    )(qi, wi, ki.astype(BF16))


def _masked_attn_kernel(q_ref, k_ref, v_ref, mask_ref, bd_ref, bo_ref, o_ref,
                        qs_ref, m_ref, l_ref, acc_ref):
    grp = N_HEADS // N_KV
    qb = pl.program_id(1)
    q = (q_ref[...] * HEAD_DIM ** -0.5).astype(BF16)
    for hh in range(grp):
        qs_ref[hh] = q[:, hh * HEAD_DIM:(hh + 1) * HEAD_DIM]
    m_ref[...] = jnp.full(m_ref.shape, NEG, F32)
    l_ref[...] = jnp.zeros(l_ref.shape, F32)
    acc_ref[...] = jnp.zeros(acc_ref.shape, F32)

    def tile(j, bias_ref):
        off = pl.multiple_of(j * ATT_TK, ATT_TK)
        kt = k_ref[pl.ds(off, ATT_TK), :]
        vt = v_ref[pl.ds(off, ATT_TK), :]
        mk = jnp.concatenate([mask_ref[0, j], mask_ref[1, j]], axis=0).astype(F32)
        for hh in range(grp):
            s = lax.dot_general(qs_ref[hh], kt, _NT, preferred_element_type=F32) + mk
            if bias_ref is not None:
                s = s + bias_ref[hh]
            m_old = m_ref[hh]
            m_new = jnp.maximum(m_old, jnp.max(s, axis=1, keepdims=True))
            alpha = jnp.exp(m_old - m_new)
            p = jnp.exp(s - jnp.concatenate([m_new] * (ATT_TK // HEAD_DIM), axis=1))
            l_ref[hh] = alpha * l_ref[hh] + jnp.sum(p, axis=1, keepdims=True)
            acc_ref[hh] = alpha * acc_ref[hh] + jnp.dot(p.astype(BF16), vt, preferred_element_type=F32)
            m_ref[hh] = m_new

    def far_tile(j, carry):
        tile(j, None)
        return carry

    lax.fori_loop(0, jnp.maximum(qb - 1, 0), far_tile, 0)

    @pl.when(qb >= 1)
    def _():
        tile(qb - 1, bo_ref)

    tile(qb, bd_ref)
    for hh in range(grp):
        o_ref[:, hh * HEAD_DIM:(hh + 1) * HEAD_DIM] = (acc_ref[hh] / l_ref[hh]).astype(o_ref.dtype)


def _rel_bias_tiles(rel_bias):
    a = jnp.arange(ATT_QB, dtype=jnp.int32)[:, None]
    c = jnp.arange(ATT_TK, dtype=jnp.int32)[None, :]
    far = rel_bias[N_BUCKETS - 1]
    diag = rel_bias[_rel_bucket(a - c)] - far
    prev = rel_bias[_rel_bucket(a - c + ATT_TK)] - far
    return diag.transpose(2, 0, 1), prev.transpose(2, 0, 1)


def _masked_attn(q, k, v, mask, rel_bias):
    s = q.shape[0]
    grp = N_HEADS // N_KV
    n_tiles = s // ATT_TK
    bd, bo = _rel_bias_tiles(rel_bias)
    wq = grp * HEAD_DIM
    return pl.pallas_call(
        _masked_attn_kernel,
        grid=(N_KV, s // ATT_QB),
        in_specs=[pl.BlockSpec((ATT_QB, wq), lambda g, i: (i, g)),
                  pl.BlockSpec((s, HEAD_DIM), lambda g, i: (0, g)),
                  pl.BlockSpec((s, HEAD_DIM), lambda g, i: (0, g)),
                  pl.BlockSpec((ATT_QB // IDX_QB, n_tiles, IDX_QB, ATT_TK), lambda g, i: (i, 0, 0, 0)),
                  pl.BlockSpec((grp, ATT_QB, ATT_TK), lambda g, i: (g, 0, 0)),
                  pl.BlockSpec((grp, ATT_QB, ATT_TK), lambda g, i: (g, 0, 0))],
        out_specs=pl.BlockSpec((ATT_QB, wq), lambda g, i: (i, g)),
        out_shape=jax.ShapeDtypeStruct((s, N_HEADS * HEAD_DIM), BF16),
        scratch_shapes=[pltpu.VMEM((grp, ATT_QB, HEAD_DIM), BF16),
                        pltpu.VMEM((grp, ATT_QB, HEAD_DIM), F32),
                        pltpu.VMEM((grp, ATT_QB, HEAD_DIM), F32),
                        pltpu.VMEM((grp, ATT_QB, HEAD_DIM), F32)],
        compiler_params=pltpu.CompilerParams(dimension_semantics=("arbitrary", "arbitrary"),
                                             vmem_limit_bytes=VMEM_LIMIT),
    )(q, k.astype(BF16), v.astype(BF16), mask, bd, bo)


def _layernorm(x, g, b):
    mu = jnp.mean(x, axis=-1, keepdims=True)
    var = jnp.mean(jnp.square(x - mu), axis=-1, keepdims=True)
    return (x - mu) * lax.rsqrt(var + LN_EPS) * g + b


def _rel_bucket(dist):
    max_exact = N_BUCKETS // 2
    d = jnp.maximum(dist, 0)
    large = max_exact + (jnp.log(jnp.maximum(d, 1).astype(F32) / max_exact)
                         / math.log(MAX_DIST / max_exact) * (N_BUCKETS - max_exact)).astype(jnp.int32)
    large = jnp.minimum(large, N_BUCKETS - 1)
    return jnp.where(d < max_exact, d, large)


def _causal_conv(x, buf, w, b):
    T = x.shape[1]
    xc = jnp.concatenate([buf, x], axis=1)
    y = b
    for j in range(CONV_W):
        y = y + w[j] * xc[:, j:j + T]
    return y, xc[:, T:]


def _rglru(xr, yr, conv_buf, h0, conv_w, conv_b, w_rg_a, b_rg_a, w_rg_i, b_rg_i, lru_lambda):
    bsz, T, _ = xr.shape
    xf, new_buf = _causal_conv(xr, conv_buf, conv_w, conv_b)
    xb = xf.reshape(bsz, T, RNN_BLOCKS, RNN_BW)
    r = jax.nn.sigmoid(jnp.einsum('btnc,ncd->btnd', xb, w_rg_a) + b_rg_a).reshape(bsz, T, D_RNN)
    i = jax.nn.sigmoid(jnp.einsum('btnc,ncd->btnd', xb, w_rg_i) + b_rg_i).reshape(bsz, T, D_RNN)
    log_a = -LRU_C * r * jax.nn.softplus(-lru_lambda)
    a = jnp.exp(log_a)
    u = jnp.sqrt(-jnp.expm1(2.0 * log_a)) * (i * xf)
    u = u.at[:, 0].add(a[:, 0] * h0)

    def combine(left, right):
        return left[0] * right[0], right[0] * left[1] + right[1]

    _, h = lax.associative_scan(combine, (a, u), axis=1)
    return h * jax.nn.gelu(yr), new_buf, h[:, -1]


def _sparse_attn(q, qi, wi, qpos, kidx, gather_kv, ktop, rel_bias):
    bsz, nq = q.shape[:2]
    L = kidx.shape[1]
    s = jnp.einsum('bqhd,bld->bqhl', qi, kidx) * IDX_DIM ** -0.5
    score = jnp.einsum('bqh,bqhl->bql', wi * IDX_HEADS ** -0.5, jax.nn.relu(s))
    kpos = jnp.arange(L, dtype=jnp.int32)
    score = jnp.where(kpos[None, None, :] <= qpos[None, :, None], score, -jnp.inf)
    _, sel = lax.top_k(score, ktop)
    ks, vs = gather_kv(sel)
    grp = N_HEADS // N_KV
    qg = q.reshape(bsz, nq, N_KV, grp, HEAD_DIM)
    logits = jnp.einsum('bqkgd,bqjkd->bqkgj', qg, ks) * HEAD_DIM ** -0.5
    dist = qpos[None, :, None] - sel
    bias = rel_bias[_rel_bucket(dist)]
    bias = bias.reshape(bsz, nq, ktop, N_KV, grp).transpose(0, 1, 3, 4, 2)
    logits = jnp.where((dist >= 0)[:, :, None, None, :], logits + bias, -jnp.inf)
    p = jax.nn.softmax(logits, axis=-1)
    o = jnp.einsum('bqkgj,bqjkd->bqkgd', p, vs)
    return o.reshape(bsz, nq, N_HEADS * HEAD_DIM)


def _take_rows(a, idx):
    return jax.vmap(lambda t, i: t[i])(a, idx)


def _prompt_attn(q, k, v, qi, ki, wi, rel_bias):
    bsz, S = q.shape[:2]
    ktop = min(TOPK_MAX, S // 4)

    def gather(sel):
        return _take_rows(k, sel), _take_rows(v, sel)

    def block(b):
        q0 = b * Q_BLOCK
        sl = lambda t: lax.dynamic_slice_in_dim(t, q0, Q_BLOCK, axis=1)
        qpos = q0 + jnp.arange(Q_BLOCK, dtype=jnp.int32)
        return _sparse_attn(sl(q), sl(qi), sl(wi), qpos, ki, gather, ktop, rel_bias)

    out = lax.map(block, jnp.arange(S // Q_BLOCK))
    return out.transpose(1, 0, 2, 3).reshape(bsz, S, N_HEADS * HEAD_DIM)


def _sample_attn(q, k, v, qi, ki, wi, cache_k, cache_v, cache_idx_k, page_table, rel_bias):
    bsz, T = q.shape[:2]
    past = page_table.shape[1] * PAGE_SIZE
    ktop = min(TOPK_MAX, (past + T) // 4)
    ki_past = cache_idx_k[page_table].reshape(bsz, past, IDX_DIM)
    kidx = jnp.concatenate([ki_past, ki], axis=1)

    def gather(sel):
        is_new = (sel >= past)[..., None, None]
        ps = jnp.minimum(sel, past - 1)
        phys = _take_rows(page_table, ps // PAGE_SIZE)
        off = ps % PAGE_SIZE
        ns = jnp.clip(sel - past, 0, T - 1)
        kk = jnp.where(is_new, _take_rows(k, ns), cache_k[phys, off])
        vv = jnp.where(is_new, _take_rows(v, ns), cache_v[phys, off])
        return kk, vv

    qpos = past + jnp.arange(T, dtype=jnp.int32)
    return _sparse_attn(q, qi, wi, qpos, kidx, gather, ktop, rel_bias)


def _route(logits, router_bias):
    n = logits.shape[0]
    s = jax.nn.sigmoid(logits)
    sb = s + router_bias
    gscore = lax.top_k(sb.reshape(n, N_GROUPS, N_EXPERTS // N_GROUPS), 2)[0].sum(-1)
    _, gidx = lax.top_k(gscore, TOPK_GROUPS)
    gmask = jax.nn.one_hot(gidx, N_GROUPS, dtype=F32).sum(1) > 0
    emask = jnp.repeat(gmask, N_EXPERTS // N_GROUPS, axis=1)
    _, eidx = lax.top_k(jnp.where(emask, sb, -jnp.inf), TOP_K)
    gw = jnp.take_along_axis(s, eidx, axis=1)
    gw = gw / jnp.sum(gw, axis=-1, keepdims=True) * ROUTED_SCALE
    a = n * TOP_K
    hit = jnp.sum((eidx[:, :, None] == jnp.arange(N_EXPERTS, dtype=eidx.dtype)).astype(jnp.int32), axis=1)
    incl = jnp.cumsum(hit, axis=0)
    counts = incl[-1]
    pcounts = (counts + MOE_MB - 1) // MOE_MB * MOE_MB
    pend = jnp.cumsum(pcounts)
    pstart = pend - pcounts
    dest = jnp.take_along_axis(incl - hit + pstart[None, :], eidx, axis=1).astype(jnp.int32).reshape(a)
    slot_tok = jnp.full((MOE_NB * MOE_MB,), n, jnp.int32).at[dest].set(
        jnp.arange(a, dtype=jnp.int32) // TOP_K)
    blk_e = jnp.minimum(
        jnp.searchsorted(pend, jnp.arange(MOE_NB, dtype=jnp.int32) * MOE_MB, side='right'),
        N_EXPERTS - 1).astype(jnp.int32)
    return slot_tok, dest.reshape(n, TOP_K), gw, blk_e


def kernel(x_prompt, x_sample, cache_k, cache_v, cache_idx_k, state_conv, state_rnn, page_table,
           p_prompt, p_sample, rel_bias, w_in, conv_w, conv_b, w_rg_a, b_rg_a, w_rg_i, b_rg_i,
           lru_lambda, w_rnn_out, w_att_out, w_out, ln1_g, ln1_b, w_router, router_bias,
           w_e1, w_e3, w_e2, w_s1, w_s3, w_s2, ln2_g, ln2_b, w_ple_gate, w_ple_proj):
    ns = DEC_BATCH * DEC_SEQ
    x_all = jnp.concatenate([x_prompt[0], x_sample.reshape(ns, D_MODEL)], axis=0)
    xb = x_all.astype(BF16)
    win = w_in[0]
    offs = [0]
    for w in SPLITS:
        offs.append(offs[-1] + w)
    xr, yr, q, k, v, qi = [_mm(xb, win, offs[g], SPLITS[g]) for g in range(6)]
    w_small = jnp.pad(win[:, offs[6]:offs[8]], ((0, 0), (0, 128 - IDX_DIM - IDX_HEADS)))
    kw = _mm(xb, w_small, tn=128)
    ki, wi = kw[:, :IDX_DIM], kw[:, IDX_DIM:IDX_DIM + IDX_HEADS]
    w_gate = win[:, offs[8]:]
    g_rnn = _mm(xb, w_gate, 0, D_MODEL)
    g_att = _mm(xb, w_gate, D_MODEL, D_MODEL)

    sp = lambda t: (t[:SEQ][None], t[SEQ:].reshape(DEC_BATCH, DEC_SEQ, -1))
    lw = (conv_w[0], conv_b[0], w_rg_a[0], b_rg_a[0], w_rg_i[0], b_rg_i[0], lru_lambda[0])
    xr_p, xr_s = sp(xr)
    yr_p, yr_s = sp(yr)
    rnn_p, buf_p, h_p = _rglru(xr_p, yr_p, jnp.zeros((1, CONV_W - 1, D_RNN), F32),
                               jnp.zeros((1, D_RNN), F32), *lw)
    rnn_s, buf_s, h_s = _rglru(xr_s, yr_s, state_conv[0], state_rnn[0], *lw)

    q_p, q_s = sp(q)
    k_p, k_s = sp(k)
    v_p, v_s = sp(v)
    qi_p, qi_s = sp(qi)
    ki_p, ki_s = sp(ki)
    wi_p, wi_s = sp(wi)
    hd = lambda t, n, d: t.reshape(t.shape[0], t.shape[1], n, d)
    k_p, k_s = hd(k_p, N_KV, HEAD_DIM), hd(k_s, N_KV, HEAD_DIM)
    v_p, v_s = hd(v_p, N_KV, HEAD_DIM), hd(v_s, N_KV, HEAD_DIM)
    att_p = _masked_attn(q[:SEQ], k[:SEQ], v[:SEQ], _index_select(qi[:SEQ], ki[:SEQ], wi[:SEQ]),
                         rel_bias)
    att_s = _sample_attn(hd(q_s, N_HEADS, HEAD_DIM), k_s, v_s, hd(qi_s, IDX_HEADS, IDX_DIM),
                         ki_s, wi_s, cache_k[0], cache_v[0], cache_idx_k[0], page_table, rel_bias)

    rnn_o = jnp.concatenate([rnn_p[0], rnn_s.reshape(ns, D_RNN)], axis=0).astype(BF16)
    att_o = jnp.concatenate([att_p, att_s.reshape(ns, D_MODEL).astype(BF16)], axis=0)
    merged = (jax.nn.sigmoid(g_rnn) * _mm(rnn_o, w_rnn_out[0])
              + jax.nn.sigmoid(g_att) * _mm(att_o, w_att_out[0]))
    x1 = _layernorm(DN_ALPHA * x_all + _mm(merged.astype(BF16), w_out[0]), ln1_g[0], ln1_b[0])

    w_router_p = jnp.pad(w_router[0], ((0, 0), (0, 128 - N_EXPERTS)))
    logits = _mm_f32(x1, w_router_p)[:, :N_EXPERTS]
    slot_tok, dest, gw, blk_e = _route(logits, router_bias[0])
    x1b = x1.astype(BF16)
    xpad = jnp.concatenate([x1b, jnp.zeros((1, D_MODEL), BF16)], axis=0)
    yb = _experts(xpad[slot_tok], blk_e, w_e1[0], w_e3[0], w_e2[0])
    routed = jnp.sum(yb[dest] * gw[:, :, None], axis=1)
    hs = _mm(x1b, w_s1[0])
    hs = (hs * jax.nn.sigmoid(hs) * _mm(x1b, w_s3[0])).astype(BF16)
    moe = routed + _mm(hs, w_s2[0])
    x2 = _layernorm(DN_ALPHA * x1 + moe, ln2_g[0], ln2_b[0])

    p_all = jnp.concatenate([p_prompt[0, 0], p_sample[0].reshape(ns, -1)], axis=0).astype(BF16)
    y = x2 + jax.nn.sigmoid(_mm(x2.astype(BF16), w_ple_gate[0])) * _mm(p_all, w_ple_proj[0])

    return (y[:SEQ][None], y[SEQ:].reshape(DEC_BATCH, DEC_SEQ, D_MODEL),
            k_p[None], v_p[None], ki_p[None], buf_p[None], h_p[None],
            k_s[None], v_s[None], ki_s[None], buf_s[None], h_s[None])
```

```python
import functools
import math

import jax
import jax.numpy as jnp
from jax import lax
from jax.experimental import pallas as pl
from jax.experimental.pallas import tpu as pltpu

D_MODEL = 4096
SEQ = 8192
DEC_BATCH = 32
DEC_SEQ = 4
PAGE_SIZE = 128
D_RNN = D_MODEL
RNN_BLOCKS = 16
RNN_BW = D_RNN // RNN_BLOCKS
CONV_W = 4
LRU_C = 8.0
N_HEADS = 32
HEAD_DIM = D_MODEL // N_HEADS
N_KV = 8
IDX_HEADS = 16
IDX_DIM = 64
TOPK_MAX = 256
Q_BLOCK = 128
N_BUCKETS = 32
MAX_DIST = 128
N_EXPERTS = 64
TOP_K = 8
N_GROUPS = 8
TOPK_GROUPS = 4
D_EXPERT = 1024
ROUTED_SCALE = 2.5
DN_ALPHA = 2.0 ** 0.25
LN_EPS = 1e-5
SPLITS = (D_RNN, D_RNN, N_HEADS * HEAD_DIM, N_KV * HEAD_DIM, N_KV * HEAD_DIM,
          IDX_HEADS * IDX_DIM, IDX_DIM, IDX_HEADS, D_MODEL, D_MODEL)

F32 = jnp.float32
BF16 = jnp.bfloat16

N_TOK = SEQ + DEC_BATCH * DEC_SEQ
MM_TM = 640
MM_TN = 512
MOE_MB = 256
MOE_NB = -(-(N_TOK * TOP_K) // MOE_MB) + N_EXPERTS
MOE_CE = 256
VMEM_LIMIT = 48 * 1024 * 1024


def _mm_kernel(x_ref, w_ref, o_ref, wbf_ref):
    @pl.when(pl.program_id(1) == 0)
    def _():
        wbf_ref[...] = w_ref[...].astype(BF16)

    o_ref[...] = jnp.dot(x_ref[...], wbf_ref[...], preferred_element_type=F32).astype(o_ref.dtype)


def _mm(x, w, col0=0, ncols=None, out_dtype=F32, tm=MM_TM, tn=MM_TN):
    m, k = x.shape
    ncols = w.shape[1] - col0 if ncols is None else ncols
    tn = min(tn, ncols)
    assert m % tm == 0 and ncols % tn == 0 and col0 % tn == 0
    cb = col0 // tn
    return pl.pallas_call(
        _mm_kernel,
        grid=(ncols // tn, m // tm),
        in_specs=[pl.BlockSpec((tm, k), lambda j, i: (i, 0)),
                  pl.BlockSpec((k, tn), lambda j, i: (0, j + cb))],
        out_specs=pl.BlockSpec((tm, tn), lambda j, i: (i, j)),
        out_shape=jax.ShapeDtypeStruct((m, ncols), out_dtype),
        scratch_shapes=[pltpu.VMEM((k, tn), BF16)],
        compiler_params=pltpu.CompilerParams(dimension_semantics=("arbitrary", "arbitrary"),
                                             vmem_limit_bytes=VMEM_LIMIT),
    )(x, w)


def _mm_f32_kernel(x_ref, w_ref, o_ref):
    o_ref[...] = jnp.dot(x_ref[...], w_ref[...], preferred_element_type=F32,
                         precision=lax.Precision.HIGHEST)


def _mm_f32(x, w, tm=MM_TM):
    m, k = x.shape
    n = w.shape[1]
    return pl.pallas_call(
        _mm_f32_kernel,
        grid=(m // tm,),
        in_specs=[pl.BlockSpec((tm, k), lambda i: (i, 0)),
                  pl.BlockSpec((k, n), lambda i: (0, 0))],
        out_specs=pl.BlockSpec((tm, n), lambda i: (i, 0)),
        out_shape=jax.ShapeDtypeStruct((m, n), F32),
        compiler_params=pltpu.CompilerParams(dimension_semantics=("arbitrary",),
                                             vmem_limit_bytes=VMEM_LIMIT),
    )(x, w)


def _expert_up_kernel(be_ref, x_ref, w1_ref, w3_ref, h_ref, w1bf_ref, w3bf_ref):
    b = pl.program_id(1)
    prev = be_ref[jnp.maximum(b - 1, 0)]

    @pl.when((b == 0) | (be_ref[b] != prev))
    def _():
        w1bf_ref[...] = w1_ref[0].astype(BF16)
        w3bf_ref[...] = w3_ref[0].astype(BF16)

    x = x_ref[...]
    a = jnp.dot(x, w1bf_ref[...], preferred_element_type=F32)
    g = jnp.dot(x, w3bf_ref[...], preferred_element_type=F32)
    h_ref[...] = (a * jax.nn.sigmoid(a) * g).astype(h_ref.dtype)


def _expert_down_kernel(be_ref, h_ref, w2_ref, y_ref, w2bf_ref):
    b = pl.program_id(1)
    prev = be_ref[jnp.maximum(b - 1, 0)]

    @pl.when((b == 0) | (be_ref[b] != prev))
    def _():
        w2bf_ref[...] = w2_ref[0].astype(BF16)

    y_ref[...] = jnp.dot(h_ref[...], w2bf_ref[...], preferred_element_type=F32)


def _experts(xs, blk_e, w_e1, w_e3, w_e2):
    d = xs.shape[1]
    nce = D_EXPERT // MOE_CE
    h = pl.pallas_call(
        _expert_up_kernel,
        grid_spec=pltpu.PrefetchScalarGridSpec(
            num_scalar_prefetch=1,
            grid=(nce, MOE_NB),
            in_specs=[pl.BlockSpec((MOE_MB, d), lambda c, b, be: (b, 0)),
                      pl.BlockSpec((1, d, MOE_CE), lambda c, b, be: (be[b], 0, c)),
                      pl.BlockSpec((1, d, MOE_CE), lambda c, b, be: (be[b], 0, c))],
            out_specs=pl.BlockSpec((MOE_MB, MOE_CE), lambda c, b, be: (b, c)),
            scratch_shapes=[pltpu.VMEM((d, MOE_CE), BF16), pltpu.VMEM((d, MOE_CE), BF16)]),
        out_shape=jax.ShapeDtypeStruct((MOE_NB * MOE_MB, D_EXPERT), BF16),
        compiler_params=pltpu.CompilerParams(dimension_semantics=("arbitrary", "arbitrary"),
                                             vmem_limit_bytes=VMEM_LIMIT),
    )(blk_e, xs, w_e1, w_e3)
    tn = 1024
    return pl.pallas_call(
        _expert_down_kernel,
        grid_spec=pltpu.PrefetchScalarGridSpec(
            num_scalar_prefetch=1,
            grid=(d // tn, MOE_NB),
            in_specs=[pl.BlockSpec((MOE_MB, D_EXPERT), lambda c, b, be: (b, 0)),
                      pl.BlockSpec((1, D_EXPERT, tn), lambda c, b, be: (be[b], 0, c))],
            out_specs=pl.BlockSpec((MOE_MB, tn), lambda c, b, be: (b, c)),
            scratch_shapes=[pltpu.VMEM((D_EXPERT, tn), BF16)]),
        out_shape=jax.ShapeDtypeStruct((MOE_NB * MOE_MB, d), F32),
        compiler_params=pltpu.CompilerParams(dimension_semantics=("arbitrary", "arbitrary"),
                                             vmem_limit_bytes=VMEM_LIMIT),
    )(blk_e, h, w_e2)


LANES = 128


def _router_kernel(x_ref, w_ref, b_ref, eidx_ref, gw_ref, hit_ref):
    gsz = N_EXPERTS // N_GROUPS
    logits = jnp.dot(x_ref[...], w_ref[...], preferred_element_type=F32, precision=lax.Precision.HIGHEST)
    s = jax.nn.sigmoid(logits)
    sb = s + b_ref[...]
    lane = lax.broadcasted_iota(jnp.int32, sb.shape, 1)

    def group_reduce(v, op):
        step = 1
        while step < gsz:
            partner = jnp.where((lane & step) != 0, pltpu.roll(v, step, axis=1),
                                pltpu.roll(v, LANES - step, axis=1))
            v = op(v, partner)
            step *= 2
        return v

    max1 = group_reduce(sb, jnp.maximum)
    first = group_reduce(jnp.where(sb == max1, lane, LANES), jnp.minimum)
    max2 = group_reduce(jnp.where(lane == first, -jnp.inf, sb), jnp.maximum)
    gscore = max1 + max2
    gi = (lane % N_EXPERTS) // gsz
    rank = jnp.zeros(sb.shape, jnp.int32)
    for k in range(1, N_GROUPS):
        other = pltpu.roll(gscore, k * gsz, axis=1)
        oi = (gi - k) % N_GROUPS
        rank = rank + jnp.where((other > gscore) | ((other == gscore) & (oi < gi)), 1, 0)
    cand = jnp.where((rank < TOPK_GROUPS) & (lane < N_EXPERTS), sb, -jnp.inf)
    eidx = jnp.zeros(sb.shape, jnp.int32)
    gw = jnp.zeros(sb.shape, F32)
    hit = jnp.zeros(sb.shape, jnp.int32)
    for r in range(TOP_K):
        best = jnp.max(cand, axis=1, keepdims=True)
        idx = jnp.min(jnp.where(cand == best, lane, LANES), axis=1, keepdims=True)
        chosen = lane == idx
        val = jnp.sum(jnp.where(chosen, s, 0.0), axis=1, keepdims=True)
        eidx = jnp.where(lane == r, idx, eidx)
        gw = jnp.where(lane == r, val, gw)
        hit = jnp.where(chosen, 1, hit)
        cand = jnp.where(chosen, -jnp.inf, cand)
    eidx_ref[...] = eidx
    gw_ref[...] = gw / jnp.sum(gw, axis=1, keepdims=True) * ROUTED_SCALE
    hit_ref[...] = hit


def _router(x, w_router, router_bias, tm=MM_TM):
    n, d = x.shape
    out_spec = pl.BlockSpec((tm, LANES), lambda i: (i, 0))
    eidx, gw, hit = pl.pallas_call(
        _router_kernel,
        grid=(n // tm,),
        in_specs=[pl.BlockSpec((tm, d), lambda i: (i, 0)),
                  pl.BlockSpec((d, LANES), lambda i: (0, 0)),
                  pl.BlockSpec((1, LANES), lambda i: (0, 0))],
        out_specs=[out_spec, out_spec, out_spec],
        out_shape=[jax.ShapeDtypeStruct((n, LANES), jnp.int32), jax.ShapeDtypeStruct((n, LANES), F32),
                   jax.ShapeDtypeStruct((n, LANES), jnp.int32)],
        compiler_params=pltpu.CompilerParams(dimension_semantics=("arbitrary",),
                                             vmem_limit_bytes=VMEM_LIMIT),
    )(x, jnp.concatenate([w_router, w_router], axis=1),
      jnp.concatenate([router_bias, router_bias]).reshape(1, LANES))
    return eidx[:, :TOP_K], gw[:, :TOP_K], hit[:, :N_EXPERTS]


IDX_QB = 128
ATT_TK = 256
ATT_QB = 256
NEG = -1e30
I32_MIN = -2 ** 31
_NT = (((1,), (1,)), ((), ()))


def _index_select_kernel(qi_ref, wi_ref, ki_ref, mask_ref, skey_ref, wb_ref, qh_ref):
    b = pl.program_id(0)
    n_tiles = mask_ref.shape[1]
    nt = (b * IDX_QB + IDX_QB + ATT_TK - 1) // ATT_TK
    qi = (qi_ref[...] * IDX_DIM ** -0.5).astype(BF16)
    w = wi_ref[...] * IDX_HEADS ** -0.5
    for h in range(IDX_HEADS):
        qh_ref[h] = qi[:, h * IDX_DIM:(h + 1) * IDX_DIM]
        wb_ref[h] = jnp.broadcast_to(w[:, h:h + 1], (IDX_QB, ATT_TK))
    row = b * IDX_QB + lax.broadcasted_iota(jnp.int32, (IDX_QB, ATT_TK), 0)
    col = lax.broadcasted_iota(jnp.int32, (IDX_QB, ATT_TK), 1)

    def score_tile(j, carry):
        kt = ki_ref[pl.ds(pl.multiple_of(j * ATT_TK, ATT_TK), ATT_TK), :]
        acc = jnp.zeros((IDX_QB, ATT_TK), F32)
        for h in range(IDX_HEADS):
            s = lax.dot_general(qh_ref[h], kt, _NT, preferred_element_type=F32)
            acc = acc + wb_ref[h] * jnp.maximum(s, 0.0)
        bits = pltpu.bitcast(acc, jnp.int32)
        key = bits ^ ((bits >> 31) & 0x7FFFFFFF)
        skey_ref[j] = jnp.where(col + j * ATT_TK <= row, key, I32_MIN)
        return carry

    lax.fori_loop(0, nt, score_tile, 0)

    half = ATT_TK // 2

    def bit_body(i, ans):
        cand_u = ans | lax.shift_left(jnp.int32(1), 31 - i)
        cand = jnp.broadcast_to(cand_u ^ I32_MIN, (IDX_QB, half))

        def count(j, c):
            t = skey_ref[j]
            return (c + jnp.where(t[:, :half] >= cand, 1, 0) + jnp.where(t[:, half:] >= cand, 1, 0))

        c = lax.fori_loop(0, nt, count, jnp.zeros((IDX_QB, half), jnp.int32))
        cnt = jnp.sum(c, axis=1, keepdims=True)
        return jnp.where(cnt >= TOPK_MAX, cand_u, ans)

    ans = lax.fori_loop(0, 32, bit_body, jnp.zeros((IDX_QB, 1), jnp.int32))
    thr = jnp.broadcast_to(jnp.maximum(ans ^ I32_MIN, I32_MIN + 1), (IDX_QB, ATT_TK))

    def write_tile(j, carry):
        mask_ref[0, j] = jnp.where(skey_ref[j] >= thr, 0.0, NEG).astype(BF16)
        return carry

    def write_empty(j, carry):
        mask_ref[0, j] = jnp.full((IDX_QB, ATT_TK), NEG, BF16)
        return carry

    lax.fori_loop(0, nt, write_tile, 0)
    lax.fori_loop(nt, n_tiles, write_empty, 0)


def _index_select(qi, ki, wi):
    s = qi.shape[0]
    n_tiles = s // ATT_TK
    return pl.pallas_call(
        _index_select_kernel,
        grid=(s // IDX_QB,),
        in_specs=[pl.BlockSpec((IDX_QB, IDX_HEADS * IDX_DIM), lambda b: (b, 0)),
                  pl.BlockSpec((IDX_QB, IDX_HEADS), lambda b: (b, 0)),
                  pl.BlockSpec((s, IDX_DIM), lambda b: (0, 0))],
        out_specs=pl.BlockSpec((1, n_tiles, IDX_QB, ATT_TK), lambda b: (b, 0, 0, 0)),
        out_shape=jax.ShapeDtypeStruct((s // IDX_QB, n_tiles, IDX_QB, ATT_TK), BF16),
        scratch_shapes=[pltpu.VMEM((n_tiles, IDX_QB, ATT_TK), jnp.int32),
                        pltpu.VMEM((IDX_HEADS, IDX_QB, ATT_TK), F32),
                        pltpu.VMEM((IDX_HEADS, IDX_QB, IDX_DIM), BF16)],
        compiler_params=pltpu.CompilerParams(dimension_semantics=("arbitrary",),
                                             vmem_limit_bytes=VMEM_LIMIT),
    )(qi, wi, ki.astype(BF16))


def _masked_attn_kernel(q_ref, k_ref, v_ref, mask_ref, bd_ref, bo_ref, o_ref,
                        qs_ref, m_ref, l_ref, acc_ref):
    grp = N_HEADS // N_KV
    qb = pl.program_id(1)
    q = (q_ref[...] * HEAD_DIM ** -0.5).astype(BF16)
    for hh in range(grp):
        qs_ref[hh] = q[:, hh * HEAD_DIM:(hh + 1) * HEAD_DIM]
    m_ref[...] = jnp.full(m_ref.shape, NEG, F32)
    l_ref[...] = jnp.zeros(l_ref.shape, F32)
    acc_ref[...] = jnp.zeros(acc_ref.shape, F32)

    def tile(j, bias_ref):
        off = pl.multiple_of(j * ATT_TK, ATT_TK)
        kt = k_ref[pl.ds(off, ATT_TK), :]
        vt = v_ref[pl.ds(off, ATT_TK), :]
        mk = jnp.concatenate([mask_ref[0, j], mask_ref[1, j]], axis=0).astype(F32)
        for hh in range(grp):
            s = lax.dot_general(qs_ref[hh], kt, _NT, preferred_element_type=F32) + mk
            if bias_ref is not None:
                s = s + bias_ref[hh]
            m_old = m_ref[hh]
            m_new = jnp.maximum(m_old, jnp.max(s, axis=1, keepdims=True))
            alpha = jnp.exp(m_old - m_new)
            p = jnp.exp(s - jnp.concatenate([m_new] * (ATT_TK // HEAD_DIM), axis=1))
            l_ref[hh] = alpha * l_ref[hh] + jnp.sum(p, axis=1, keepdims=True)
            acc_ref[hh] = alpha * acc_ref[hh] + jnp.dot(p.astype(BF16), vt, preferred_element_type=F32)
            m_ref[hh] = m_new

    def far_tile(j, carry):
        tile(j, None)
        return carry

    lax.fori_loop(0, jnp.maximum(qb - 1, 0), far_tile, 0)

    @pl.when(qb >= 1)
    def _():
        tile(qb - 1, bo_ref)

    tile(qb, bd_ref)
    for hh in range(grp):
        o_ref[:, hh * HEAD_DIM:(hh + 1) * HEAD_DIM] = (acc_ref[hh] / l_ref[hh]).astype(o_ref.dtype)


def _rel_bias_tiles(rel_bias):
    a = jnp.arange(ATT_QB, dtype=jnp.int32)[:, None]
    c = jnp.arange(ATT_TK, dtype=jnp.int32)[None, :]
    far = rel_bias[N_BUCKETS - 1]
    diag = rel_bias[_rel_bucket(a - c)] - far
    prev = rel_bias[_rel_bucket(a - c + ATT_TK)] - far
    return diag.transpose(2, 0, 1), prev.transpose(2, 0, 1)


def _masked_attn(q, k, v, mask, rel_bias):
    s = q.shape[0]
    grp = N_HEADS // N_KV
    n_tiles = s // ATT_TK
    bd, bo = _rel_bias_tiles(rel_bias)
    wq = grp * HEAD_DIM
    return pl.pallas_call(
        _masked_attn_kernel,
        grid=(N_KV, s // ATT_QB),
        in_specs=[pl.BlockSpec((ATT_QB, wq), lambda g, i: (i, g)),
                  pl.BlockSpec((s, HEAD_DIM), lambda g, i: (0, g)),
                  pl.BlockSpec((s, HEAD_DIM), lambda g, i: (0, g)),
                  pl.BlockSpec((ATT_QB // IDX_QB, n_tiles, IDX_QB, ATT_TK), lambda g, i: (i, 0, 0, 0)),
                  pl.BlockSpec((grp, ATT_QB, ATT_TK), lambda g, i: (g, 0, 0)),
                  pl.BlockSpec((grp, ATT_QB, ATT_TK), lambda g, i: (g, 0, 0))],
        out_specs=pl.BlockSpec((ATT_QB, wq), lambda g, i: (i, g)),
        out_shape=jax.ShapeDtypeStruct((s, N_HEADS * HEAD_DIM), BF16),
        scratch_shapes=[pltpu.VMEM((grp, ATT_QB, HEAD_DIM), BF16),
                        pltpu.VMEM((grp, ATT_QB, HEAD_DIM), F32),
                        pltpu.VMEM((grp, ATT_QB, HEAD_DIM), F32),
                        pltpu.VMEM((grp, ATT_QB, HEAD_DIM), F32)],
        compiler_params=pltpu.CompilerParams(dimension_semantics=("arbitrary", "arbitrary"),
                                             vmem_limit_bytes=VMEM_LIMIT),
    )(q, k.astype(BF16), v.astype(BF16), mask, bd, bo)


SMP_PG = 8
SMP_TP = 8


def _order_key(score):
    bits = pltpu.bitcast(score, jnp.int32)
    return bits ^ ((bits >> 31) & 0x7FFFFFFF)


def _sample_select_kernel(pt_ref, *refs):
    pages = refs[:SMP_PG]
    kn_ref, qi_ref, w_ref, skey_ref, thr_ref = refs[SMP_PG:]
    p = pl.program_id(1)
    n_tiles = skey_ref.shape[1]
    qi = qi_ref[0]
    w = w_ref[0].reshape(IDX_HEADS, SMP_TP, PAGE_SIZE)

    def score(kt):
        s = lax.dot_general(qi, kt.astype(BF16), _NT, preferred_element_type=F32)
        s = jnp.maximum(s, 0.0).reshape(IDX_HEADS, SMP_TP, PAGE_SIZE)
        return _order_key(jnp.sum(w * s, axis=0))

    for i in range(SMP_PG):
        skey_ref[0, p * SMP_PG + i] = score(pages[i][0])

    @pl.when(p == pl.num_programs(1) - 1)
    def _():
        t = lax.broadcasted_iota(jnp.int32, (SMP_TP, PAGE_SIZE), 0)
        j = lax.broadcasted_iota(jnp.int32, (SMP_TP, PAGE_SIZE), 1)
        skey_ref[0, n_tiles - 1] = jnp.where((j <= t) & (j < DEC_SEQ), score(kn_ref[0]), I32_MIN)

        def bit_body(i, ans):
            cand_u = ans | lax.shift_left(jnp.int32(1), 31 - i)
            cand = jnp.broadcast_to(cand_u ^ I32_MIN, (SMP_TP, PAGE_SIZE))

            def count(jt, c):
                return c + jnp.where(skey_ref[0, jt] >= cand, 1, 0)

            c = lax.fori_loop(0, n_tiles, count, jnp.zeros((SMP_TP, PAGE_SIZE), jnp.int32))
            return jnp.where(jnp.sum(c, axis=1, keepdims=True) >= TOPK_MAX, cand_u, ans)

        ans = lax.fori_loop(0, 32, bit_body, jnp.zeros((SMP_TP, 1), jnp.int32))
        thr_ref[0] = jnp.broadcast_to(jnp.maximum(ans ^ I32_MIN, I32_MIN + 1), (SMP_TP, PAGE_SIZE))


def _sample_attn_kernel(pt_ref, *refs):
    kp, vp = refs[:SMP_PG], refs[SMP_PG:2 * SMP_PG]
    kn_ref, vn_ref, q_ref, skey_ref, thr_ref, bl_ref, bn_ref, o_ref, m_ref, l_ref, acc_ref = refs[2 * SMP_PG:]
    grp = N_HEADS // N_KV
    p = pl.program_id(1)
    last = p == pl.num_programs(1) - 1
    n_tiles = skey_ref.shape[1]

    @pl.when(p == 0)
    def _():
        m_ref[...] = jnp.full(m_ref.shape, NEG, F32)
        l_ref[...] = jnp.zeros(l_ref.shape, F32)
        acc_ref[...] = jnp.zeros(acc_ref.shape, F32)

    thr = thr_ref[0]

    def mask_of(tile):
        mk = jnp.where(skey_ref[0, tile] >= thr, 0.0, NEG)
        return jnp.concatenate([mk] * grp, axis=0)

    def update(g, k_g, v_g, add):
        s = lax.dot_general(q_ref[0, g], k_g, _NT, preferred_element_type=F32) + add
        m_old = m_ref[g]
        m_new = jnp.maximum(m_old, jnp.max(s, axis=1, keepdims=True))
        alpha = jnp.exp(m_old - m_new)
        pr = jnp.exp(s - jnp.concatenate([m_new] * (s.shape[1] // HEAD_DIM), axis=1))
        l_ref[g] = alpha * l_ref[g] + jnp.sum(pr, axis=1, keepdims=True)
        acc_ref[g] = alpha * acc_ref[g] + jnp.dot(pr.astype(BF16), v_g, preferred_element_type=F32)
        m_ref[g] = m_new

    mk = jnp.concatenate([mask_of(p * SMP_PG + i) for i in range(SMP_PG)], axis=1)
    near = jnp.where(last, 1.0, 0.0)
    for g in range(N_KV):
        cols = slice(g * HEAD_DIM, (g + 1) * HEAD_DIM)
        k_g = jnp.concatenate([kp[i][0, :, cols].astype(BF16) for i in range(SMP_PG)], axis=0)
        v_g = jnp.concatenate([vp[i][0, :, cols].astype(BF16) for i in range(SMP_PG)], axis=0)
        bias = jnp.concatenate([jnp.zeros((grp * SMP_TP, (SMP_PG - 1) * PAGE_SIZE), F32),
                                near * bl_ref[g]], axis=1)
        update(g, k_g, v_g, mk + bias)

    @pl.when(last)
    def _():
        mn = mask_of(n_tiles - 1)
        for g in range(N_KV):
            cols = slice(g * HEAD_DIM, (g + 1) * HEAD_DIM)
            update(g, kn_ref[0, :, cols].astype(BF16), vn_ref[0, :, cols].astype(BF16), mn + bn_ref[g])
        for g in range(N_KV):
            o_ref[0, g] = acc_ref[g] / l_ref[g]


def _sample_attn_pallas(q, k, v, qi, ki, wi, cache_k, cache_v, cache_idx_k, page_table, rel_bias):
    bsz, nq = q.shape[:2]
    n_pages = page_table.shape[1]
    past = n_pages * PAGE_SIZE
    grp = N_HEADS // N_KV
    n_steps = n_pages // SMP_PG
    tpad = ((0, 0), (0, 0), (0, SMP_TP - nq), (0, 0))
    kpad = ((0, 0), (0, PAGE_SIZE - nq), (0, 0))
    qi_r = jnp.pad((qi * IDX_DIM ** -0.5).reshape(bsz, nq, IDX_HEADS, IDX_DIM).transpose(0, 2, 1, 3), tpad)
    qi_r = qi_r.reshape(bsz, IDX_HEADS * SMP_TP, IDX_DIM).astype(BF16)
    w_r = jnp.pad((wi * IDX_HEADS ** -0.5).transpose(0, 2, 1), ((0, 0), (0, 0), (0, SMP_TP - nq)))
    w_r = jnp.broadcast_to(w_r.reshape(bsz, IDX_HEADS * SMP_TP, 1), (bsz, IDX_HEADS * SMP_TP, PAGE_SIZE))
    page_specs = lambda width: [
        pl.BlockSpec((1, PAGE_SIZE, width), functools.partial(
            lambda b, p, pt, i: (pt[b, p * SMP_PG + i], 0, 0), i=i)) for i in range(SMP_PG)]
    per_batch = lambda *shape: pl.BlockSpec((1,) + shape, lambda b, p, pt: (b,) + (0,) * len(shape))
    skey, thr = pl.pallas_call(
        _sample_select_kernel,
        grid_spec=pltpu.PrefetchScalarGridSpec(
            num_scalar_prefetch=1,
            grid=(bsz, n_steps),
            in_specs=page_specs(IDX_DIM) + [per_batch(PAGE_SIZE, IDX_DIM),
                                            per_batch(IDX_HEADS * SMP_TP, IDX_DIM),
                                            per_batch(IDX_HEADS * SMP_TP, PAGE_SIZE)],
            out_specs=[per_batch(n_pages + 1, SMP_TP, PAGE_SIZE), per_batch(SMP_TP, PAGE_SIZE)]),
        out_shape=[jax.ShapeDtypeStruct((bsz, n_pages + 1, SMP_TP, PAGE_SIZE), jnp.int32),
                   jax.ShapeDtypeStruct((bsz, SMP_TP, PAGE_SIZE), jnp.int32)],
        compiler_params=pltpu.CompilerParams(dimension_semantics=("arbitrary", "arbitrary"),
                                             vmem_limit_bytes=VMEM_LIMIT),
    )(page_table, *([cache_idx_k] * SMP_PG), jnp.pad(ki, kpad), qi_r, w_r)

    q_r = jnp.pad((q * HEAD_DIM ** -0.5).reshape(bsz, nq, N_KV, grp, HEAD_DIM).transpose(0, 2, 3, 1, 4),
                  ((0, 0), (0, 0), (0, 0), (0, SMP_TP - nq), (0, 0)))
    q_r = q_r.reshape(bsz, N_KV, grp * SMP_TP, HEAD_DIM).astype(BF16)
    t = jnp.arange(SMP_TP, dtype=jnp.int32)[:, None]
    lane = jnp.arange(PAGE_SIZE, dtype=jnp.int32)[None, :]
    rows = lambda b: b.transpose(2, 0, 1).reshape(N_KV, grp * SMP_TP, PAGE_SIZE)
    far = rel_bias[N_BUCKETS - 1]
    b_last = rows(rel_bias[_rel_bucket(PAGE_SIZE + t - lane)] - far)
    b_new = rows(rel_bias[_rel_bucket(t - lane)] - far)
    kvw = N_KV * HEAD_DIM
    const = lambda *shape: pl.BlockSpec(shape, lambda b, p, pt: (0,) * len(shape))
    o = pl.pallas_call(
        _sample_attn_kernel,
        grid_spec=pltpu.PrefetchScalarGridSpec(
            num_scalar_prefetch=1,
            grid=(bsz, n_steps),
            in_specs=page_specs(kvw) + page_specs(kvw) + [
                per_batch(PAGE_SIZE, kvw), per_batch(PAGE_SIZE, kvw),
                per_batch(N_KV, grp * SMP_TP, HEAD_DIM),
                per_batch(n_pages + 1, SMP_TP, PAGE_SIZE), per_batch(SMP_TP, PAGE_SIZE),
                const(N_KV, grp * SMP_TP, PAGE_SIZE), const(N_KV, grp * SMP_TP, PAGE_SIZE)],
            out_specs=per_batch(N_KV, grp * SMP_TP, HEAD_DIM),
            scratch_shapes=[pltpu.VMEM((N_KV, grp * SMP_TP, HEAD_DIM), F32)] * 3),
        out_shape=jax.ShapeDtypeStruct((bsz, N_KV, grp * SMP_TP, HEAD_DIM), F32),
        compiler_params=pltpu.CompilerParams(dimension_semantics=("arbitrary", "arbitrary"),
                                             vmem_limit_bytes=VMEM_LIMIT),
    )(page_table, *([cache_k.reshape(-1, PAGE_SIZE, kvw)] * SMP_PG),
      *([cache_v.reshape(-1, PAGE_SIZE, kvw)] * SMP_PG),
      jnp.pad(k, kpad), jnp.pad(v, kpad), q_r, skey, thr, b_last, b_new)
    o = o.reshape(bsz, N_KV, grp, SMP_TP, HEAD_DIM)[:, :, :, :nq]
    return o.transpose(0, 3, 1, 2, 4).reshape(bsz, nq, N_HEADS * HEAD_DIM)


LRU_TT = 512
_GELU_C = math.sqrt(2.0 / math.pi)


def _gelu_tanh(y):
    return 0.5 * y * (1.0 + jnp.tanh(_GELU_C * (y + 0.044715 * (y * y * y))))


def _lru_coeffs(xc, wa_ref, ba_ref, wi_ref, bi_ref, lam_ref):
    xb = xc.astype(BF16)
    r = jax.nn.sigmoid(jnp.dot(xb, wa_ref[0].astype(BF16), preferred_element_type=F32) + ba_ref[0])
    i = jax.nn.sigmoid(jnp.dot(xb, wi_ref[0].astype(BF16), preferred_element_type=F32) + bi_ref[0])
    z = -lam_ref[...]
    softplus = jnp.maximum(z, 0.0) + jnp.log1p(jnp.exp(-jnp.abs(z)))
    log_a = -LRU_C * r * softplus
    t = jnp.tanh(log_a)
    u = jnp.sqrt(-2.0 * t / (1.0 - t)) * (i * xc)
    return jnp.exp(log_a), u


def _rglru_prompt_kernel(x_ref, y_ref, cw_ref, cb_ref, wa_ref, ba_ref, wi_ref, bi_ref, lam_ref,
                         o_ref, nb_ref, hl_ref, xbuf_ref, hc_ref):
    c = pl.program_id(1)
    tt = x_ref.shape[0]

    @pl.when(c == 0)
    def _():
        xbuf_ref[0:8] = jnp.zeros((8, RNN_BW), F32)
        hc_ref[...] = jnp.zeros(hc_ref.shape, F32)

    x = x_ref[...]
    xbuf_ref[8:8 + tt] = x
    xc = cb_ref[...] + cw_ref[CONV_W - 1:CONV_W] * x
    for j in range(CONV_W - 1):
        xc = xc + cw_ref[j:j + 1] * xbuf_ref[8 - (CONV_W - 1) + j:8 - (CONV_W - 1) + j + tt]
    a, u = _lru_coeffs(xc, wa_ref, ba_ref, wi_ref, bi_ref, lam_ref)
    row = lax.broadcasted_iota(jnp.int32, (tt, RNN_BW), 0)
    s = 1
    while s < tt:
        keep = row >= s
        a_prev = jnp.where(keep, pltpu.roll(a, s, axis=0), 1.0)
        u_prev = jnp.where(keep, pltpu.roll(u, s, axis=0), 0.0)
        u = a * u_prev + u
        a = a * a_prev
        s *= 2
    h = a * hc_ref[0:1] + u
    o_ref[...] = (h * _gelu_tanh(y_ref[...])).astype(o_ref.dtype)
    hc_ref[0:1] = h[tt - 1:tt]
    xbuf_ref[0:8] = x[tt - 8:tt]

    @pl.when(c == pl.num_programs(1) - 1)
    def _():
        hl_ref[...] = h[tt - 1:tt]
        nb_ref[...] = x[tt - (CONV_W - 1):tt]


def _rglru_prompt(xr, yr, s, conv_w, conv_b, w_rg_a, b_rg_a, w_rg_i, b_rg_i, lru_lambda):
    vec = lambda t: t.reshape(1, D_RNN)
    blk = lambda t: t.reshape(RNN_BLOCKS, 1, RNN_BW)
    row_spec = pl.BlockSpec((LRU_TT, RNN_BW), lambda n, c: (c, n))
    vec_spec = pl.BlockSpec((1, RNN_BW), lambda n, c: (0, n))
    w_spec = pl.BlockSpec((1, RNN_BW, RNN_BW), lambda n, c: (n, 0, 0))
    b_spec = pl.BlockSpec((1, 1, RNN_BW), lambda n, c: (n, 0, 0))
    return pl.pallas_call(
        _rglru_prompt_kernel,
        grid=(RNN_BLOCKS, s // LRU_TT),
        in_specs=[row_spec, row_spec, pl.BlockSpec((CONV_W, RNN_BW), lambda n, c: (0, n)), vec_spec,
                  w_spec, b_spec, w_spec, b_spec, vec_spec],
        out_specs=[row_spec, pl.BlockSpec((CONV_W - 1, RNN_BW), lambda n, c: (0, n)), vec_spec],
        out_shape=[jax.ShapeDtypeStruct((s, D_RNN), BF16),
                   jax.ShapeDtypeStruct((CONV_W - 1, D_RNN), F32),
                   jax.ShapeDtypeStruct((1, D_RNN), F32)],
        scratch_shapes=[pltpu.VMEM((8 + LRU_TT, RNN_BW), F32), pltpu.VMEM((8, RNN_BW), F32)],
        compiler_params=pltpu.CompilerParams(dimension_semantics=("arbitrary", "arbitrary"),
                                             vmem_limit_bytes=VMEM_LIMIT),
    )(xr, yr, conv_w, vec(conv_b), w_rg_a, blk(b_rg_a), w_rg_i, blk(b_rg_i), vec(lru_lambda))


def _rglru_sample_kernel(x_ref, y_ref, buf_ref, h0_ref, cw_ref, cb_ref, wa_ref, ba_ref, wi_ref, bi_ref,
                         lam_ref, o_ref, nb_ref, hl_ref):
    nt, nb = x_ref.shape[0], x_ref.shape[1]
    xs = [buf_ref[j] for j in range(CONV_W - 1)] + [x_ref[t] for t in range(nt)]
    xc = []
    for t in range(nt):
        acc = cb_ref[...] + cw_ref[0:1] * xs[t]
        for j in range(1, CONV_W):
            acc = acc + cw_ref[j:j + 1] * xs[t + j]
        xc.append(acc)
    a, u = _lru_coeffs(jnp.concatenate(xc, axis=0), wa_ref, ba_ref, wi_ref, bi_ref, lam_ref)
    h = h0_ref[...]
    for t in range(nt):
        h = a[t * nb:(t + 1) * nb] * h + u[t * nb:(t + 1) * nb]
        o_ref[t] = (h * _gelu_tanh(y_ref[t])).astype(o_ref.dtype)
    hl_ref[...] = h
    for j in range(CONV_W - 1):
        nb_ref[j] = xs[nt + j]


def _rglru_sample(xr, yr, buf, h0, conv_w, conv_b, w_rg_a, b_rg_a, w_rg_i, b_rg_i, lru_lambda):
    nt, nb, _ = xr.shape
    vec = lambda t: t.reshape(1, D_RNN)
    blk = lambda t: t.reshape(RNN_BLOCKS, 1, RNN_BW)
    seq_spec = lambda n_rows: pl.BlockSpec((n_rows, nb, RNN_BW), lambda n: (0, 0, n))
    vec_spec = pl.BlockSpec((1, RNN_BW), lambda n: (0, n))
    w_spec = pl.BlockSpec((1, RNN_BW, RNN_BW), lambda n: (n, 0, 0))
    b_spec = pl.BlockSpec((1, 1, RNN_BW), lambda n: (n, 0, 0))
    h_spec = pl.BlockSpec((nb, RNN_BW), lambda n: (0, n))
    return pl.pallas_call(
        _rglru_sample_kernel,
        grid=(RNN_BLOCKS,),
        in_specs=[seq_spec(nt), seq_spec(nt), seq_spec(CONV_W - 1), h_spec,
                  pl.BlockSpec((CONV_W, RNN_BW), lambda n: (0, n)), vec_spec,
                  w_spec, b_spec, w_spec, b_spec, vec_spec],
        out_specs=[seq_spec(nt), seq_spec(CONV_W - 1), h_spec],
        out_shape=[jax.ShapeDtypeStruct((nt, nb, D_RNN), BF16),
                   jax.ShapeDtypeStruct((CONV_W - 1, nb, D_RNN), F32),
                   jax.ShapeDtypeStruct((nb, D_RNN), F32)],
        compiler_params=pltpu.CompilerParams(dimension_semantics=("arbitrary",),
                                             vmem_limit_bytes=VMEM_LIMIT),
    )(xr, yr, buf, h0, conv_w, vec(conv_b), w_rg_a, blk(b_rg_a), w_rg_i, blk(b_rg_i), vec(lru_lambda))


def _layernorm(x, g, b):
    mu = jnp.mean(x, axis=-1, keepdims=True)
    var = jnp.mean(jnp.square(x - mu), axis=-1, keepdims=True)
    return (x - mu) * lax.rsqrt(var + LN_EPS) * g + b


def _rel_bucket(dist):
    max_exact = N_BUCKETS // 2
    d = jnp.maximum(dist, 0)
    large = max_exact + (jnp.log(jnp.maximum(d, 1).astype(F32) / max_exact)
                         / math.log(MAX_DIST / max_exact) * (N_BUCKETS - max_exact)).astype(jnp.int32)
    large = jnp.minimum(large, N_BUCKETS - 1)
    return jnp.where(d < max_exact, d, large)


def _causal_conv(x, buf, w, b):
    T = x.shape[1]
    xc = jnp.concatenate([buf, x], axis=1)
    y = b
    for j in range(CONV_W):
        y = y + w[j] * xc[:, j:j + T]
    return y, xc[:, T:]


def _rglru(xr, yr, conv_buf, h0, conv_w, conv_b, w_rg_a, b_rg_a, w_rg_i, b_rg_i, lru_lambda):
    bsz, T, _ = xr.shape
    xf, new_buf = _causal_conv(xr, conv_buf, conv_w, conv_b)
    xb = xf.reshape(bsz, T, RNN_BLOCKS, RNN_BW)
    r = jax.nn.sigmoid(jnp.einsum('btnc,ncd->btnd', xb, w_rg_a) + b_rg_a).reshape(bsz, T, D_RNN)
    i = jax.nn.sigmoid(jnp.einsum('btnc,ncd->btnd', xb, w_rg_i) + b_rg_i).reshape(bsz, T, D_RNN)
    log_a = -LRU_C * r * jax.nn.softplus(-lru_lambda)
    a = jnp.exp(log_a)
    u = jnp.sqrt(-jnp.expm1(2.0 * log_a)) * (i * xf)
    u = u.at[:, 0].add(a[:, 0] * h0)

    def combine(left, right):
        return left[0] * right[0], right[0] * left[1] + right[1]

    _, h = lax.associative_scan(combine, (a, u), axis=1)
    return h * jax.nn.gelu(yr), new_buf, h[:, -1]


def _sparse_attn(q, qi, wi, qpos, kidx, gather_kv, ktop, rel_bias):
    bsz, nq = q.shape[:2]
    L = kidx.shape[1]
    s = jnp.einsum('bqhd,bld->bqhl', qi, kidx) * IDX_DIM ** -0.5
    score = jnp.einsum('bqh,bqhl->bql', wi * IDX_HEADS ** -0.5, jax.nn.relu(s))
    kpos = jnp.arange(L, dtype=jnp.int32)
    score = jnp.where(kpos[None, None, :] <= qpos[None, :, None], score, -jnp.inf)
    _, sel = lax.top_k(score, ktop)
    ks, vs = gather_kv(sel)
    grp = N_HEADS // N_KV
    qg = q.reshape(bsz, nq, N_KV, grp, HEAD_DIM)
    logits = jnp.einsum('bqkgd,bqjkd->bqkgj', qg, ks) * HEAD_DIM ** -0.5
    dist = qpos[None, :, None] - sel
    bias = rel_bias[_rel_bucket(dist)]
    bias = bias.reshape(bsz, nq, ktop, N_KV, grp).transpose(0, 1, 3, 4, 2)
    logits = jnp.where((dist >= 0)[:, :, None, None, :], logits + bias, -jnp.inf)
    p = jax.nn.softmax(logits, axis=-1)
    o = jnp.einsum('bqkgj,bqjkd->bqkgd', p, vs)
    return o.reshape(bsz, nq, N_HEADS * HEAD_DIM)


def _take_rows(a, idx):
    return jax.vmap(lambda t, i: t[i])(a, idx)


def _prompt_attn(q, k, v, qi, ki, wi, rel_bias):
    bsz, S = q.shape[:2]
    ktop = min(TOPK_MAX, S // 4)

    def gather(sel):
        return _take_rows(k, sel), _take_rows(v, sel)

    def block(b):
        q0 = b * Q_BLOCK
        sl = lambda t: lax.dynamic_slice_in_dim(t, q0, Q_BLOCK, axis=1)
        qpos = q0 + jnp.arange(Q_BLOCK, dtype=jnp.int32)
        return _sparse_attn(sl(q), sl(qi), sl(wi), qpos, ki, gather, ktop, rel_bias)

    out = lax.map(block, jnp.arange(S // Q_BLOCK))
    return out.transpose(1, 0, 2, 3).reshape(bsz, S, N_HEADS * HEAD_DIM)


def _sample_attn(q, k, v, qi, ki, wi, cache_k, cache_v, cache_idx_k, page_table, rel_bias):
    bsz, T = q.shape[:2]
    past = page_table.shape[1] * PAGE_SIZE
    ktop = min(TOPK_MAX, (past + T) // 4)
    ki_past = cache_idx_k[page_table].reshape(bsz, past, IDX_DIM)
    kidx = jnp.concatenate([ki_past, ki], axis=1)

    def gather(sel):
        is_new = (sel >= past)[..., None, None]
        ps = jnp.minimum(sel, past - 1)
        phys = _take_rows(page_table, ps // PAGE_SIZE)
        off = ps % PAGE_SIZE
        ns = jnp.clip(sel - past, 0, T - 1)
        kk = jnp.where(is_new, _take_rows(k, ns), cache_k[phys, off])
        vv = jnp.where(is_new, _take_rows(v, ns), cache_v[phys, off])
        return kk, vv

    qpos = past + jnp.arange(T, dtype=jnp.int32)
    return _sparse_attn(q, qi, wi, qpos, kidx, gather, ktop, rel_bias)


def _route(logits, router_bias):
    n = logits.shape[0]
    s = jax.nn.sigmoid(logits)
    sb = s + router_bias
    gscore = lax.top_k(sb.reshape(n, N_GROUPS, N_EXPERTS // N_GROUPS), 2)[0].sum(-1)
    _, gidx = lax.top_k(gscore, TOPK_GROUPS)
    gmask = jax.nn.one_hot(gidx, N_GROUPS, dtype=F32).sum(1) > 0
    emask = jnp.repeat(gmask, N_EXPERTS // N_GROUPS, axis=1)
    _, eidx = lax.top_k(jnp.where(emask, sb, -jnp.inf), TOP_K)
    gw = jnp.take_along_axis(s, eidx, axis=1)
    gw = gw / jnp.sum(gw, axis=-1, keepdims=True) * ROUTED_SCALE
    a = n * TOP_K
    hit = jnp.sum((eidx[:, :, None] == jnp.arange(N_EXPERTS, dtype=eidx.dtype)).astype(jnp.int32), axis=1)
    incl = jnp.cumsum(hit, axis=0)
    counts = incl[-1]
    pcounts = (counts + MOE_MB - 1) // MOE_MB * MOE_MB
    pend = jnp.cumsum(pcounts)
    pstart = pend - pcounts
    dest = jnp.take_along_axis(incl - hit + pstart[None, :], eidx, axis=1).astype(jnp.int32).reshape(a)
    slot_tok = jnp.full((MOE_NB * MOE_MB,), n, jnp.int32).at[dest].set(
        jnp.arange(a, dtype=jnp.int32) // TOP_K)
    blk_e = jnp.minimum(
        jnp.searchsorted(pend, jnp.arange(MOE_NB, dtype=jnp.int32) * MOE_MB, side='right'),
        N_EXPERTS - 1).astype(jnp.int32)
    return slot_tok, dest.reshape(n, TOP_K), gw, blk_e


def _dispatch(eidx, hit):
    n = eidx.shape[0]
    incl = jnp.cumsum(hit, axis=0)
    counts = incl[-1]
    pcounts = (counts + MOE_MB - 1) // MOE_MB * MOE_MB
    pend = jnp.cumsum(pcounts)
    pstart = pend - pcounts
    dest = jnp.take_along_axis(incl - hit + pstart[None, :], eidx, axis=1).astype(jnp.int32)
    slot_tok = jnp.full((MOE_NB * MOE_MB,), n, jnp.int32).at[dest.reshape(-1)].set(
        jnp.arange(n * TOP_K, dtype=jnp.int32) // TOP_K)
    blk_e = jnp.minimum(
        jnp.searchsorted(pend, jnp.arange(MOE_NB, dtype=jnp.int32) * MOE_MB, side='right'),
        N_EXPERTS - 1).astype(jnp.int32)
    return slot_tok, dest, blk_e


def kernel(x_prompt, x_sample, cache_k, cache_v, cache_idx_k, state_conv, state_rnn, page_table,
           p_prompt, p_sample, rel_bias, w_in, conv_w, conv_b, w_rg_a, b_rg_a, w_rg_i, b_rg_i,
           lru_lambda, w_rnn_out, w_att_out, w_out, ln1_g, ln1_b, w_router, router_bias,
           w_e1, w_e3, w_e2, w_s1, w_s3, w_s2, ln2_g, ln2_b, w_ple_gate, w_ple_proj):
    ns = DEC_BATCH * DEC_SEQ
    x_all = jnp.concatenate([x_prompt[0], x_sample.reshape(ns, D_MODEL)], axis=0)
    xb = x_all.astype(BF16)
    win = w_in[0]
    offs = [0]
    for w in SPLITS:
        offs.append(offs[-1] + w)
    xr, yr, q, k, v, qi = [_mm(xb, win, offs[g], SPLITS[g]) for g in range(6)]
    w_small = jnp.pad(win[:, offs[6]:offs[8]], ((0, 0), (0, 128 - IDX_DIM - IDX_HEADS)))
    kw = _mm(xb, w_small, tn=128)
    ki, wi = kw[:, :IDX_DIM], kw[:, IDX_DIM:IDX_DIM + IDX_HEADS]
    w_gate = win[:, offs[8]:]
    g_rnn = _mm(xb, w_gate, 0, D_MODEL)
    g_att = _mm(xb, w_gate, D_MODEL, D_MODEL)

    lw = (conv_w[0], conv_b[0], w_rg_a[0], b_rg_a[0], w_rg_i[0], b_rg_i[0], lru_lambda[0])
    smp = lambda t: t[SEQ:].reshape(DEC_BATCH, DEC_SEQ, -1)
    tmaj = lambda t: t.transpose(1, 0, 2)
    rnn_p, buf_p, h_p = _rglru_prompt(xr, yr, SEQ, *lw)
    rnn_s, buf_s, h_s = _rglru_sample(tmaj(smp(xr)), tmaj(smp(yr)), tmaj(state_conv[0]), state_rnn[0], *lw)
    buf_s = tmaj(buf_s)

    k_p, v_p, ki_p = k[:SEQ], v[:SEQ], ki[:SEQ]
    k_s, v_s, ki_s = smp(k), smp(v), smp(ki)
    att_p = _masked_attn(q[:SEQ], k_p, v_p, _index_select(qi[:SEQ], ki_p, wi[:SEQ]), rel_bias)
    att_s = _sample_attn_pallas(smp(q), k_s, v_s, smp(qi), ki_s, smp(wi),
                                cache_k[0], cache_v[0], cache_idx_k[0], page_table, rel_bias)
    hd = lambda t: t.reshape(t.shape[:-1] + (N_KV, HEAD_DIM))
    k_p, v_p, k_s, v_s = hd(k_p)[None], hd(v_p)[None], hd(k_s), hd(v_s)
    ki_p, buf_p = ki_p[None], buf_p[None]

    rnn_o = jnp.concatenate([rnn_p, tmaj(rnn_s).reshape(ns, D_RNN)], axis=0)
    att_o = jnp.concatenate([att_p, att_s.reshape(ns, D_MODEL).astype(BF16)], axis=0)
    merged = (jax.nn.sigmoid(g_rnn) * _mm(rnn_o, w_rnn_out[0])
              + jax.nn.sigmoid(g_att) * _mm(att_o, w_att_out[0]))
    x1 = _layernorm(DN_ALPHA * x_all + _mm(merged.astype(BF16), w_out[0]), ln1_g[0], ln1_b[0])

    eidx, gw, hit = _router(x1, w_router[0], router_bias[0])
    slot_tok, dest, blk_e = _dispatch(eidx, hit)
    x1b = x1.astype(BF16)
    xpad = jnp.concatenate([x1b, jnp.zeros((1, D_MODEL), BF16)], axis=0)
    yb = _experts(xpad[slot_tok], blk_e, w_e1[0], w_e3[0], w_e2[0])
    routed = jnp.sum(yb[dest] * gw[:, :, None], axis=1)
    hs = _mm(x1b, w_s1[0])
    hs = (hs * jax.nn.sigmoid(hs) * _mm(x1b, w_s3[0])).astype(BF16)
    moe = routed + _mm(hs, w_s2[0])
    x2 = _layernorm(DN_ALPHA * x1 + moe, ln2_g[0], ln2_b[0])

    p_all = jnp.concatenate([p_prompt[0, 0], p_sample[0].reshape(ns, -1)], axis=0).astype(BF16)
    y = x2 + jax.nn.sigmoid(_mm(x2.astype(BF16), w_ple_gate[0])) * _mm(p_all, w_ple_proj[0])

    return (y[:SEQ][None], y[SEQ:].reshape(DEC_BATCH, DEC_SEQ, D_MODEL),
            k_p[None], v_p[None], ki_p[None], buf_p[None], h_p[None],
            k_s[None], v_s[None], ki_s[None], buf_s[None], h_s[None])
```

```python
import functools
import math

import jax
import jax.numpy as jnp
from jax import lax
from jax.experimental import pallas as pl
from jax.experimental.pallas import tpu as pltpu

D_MODEL = 4096
SEQ = 8192
DEC_BATCH = 32
DEC_SEQ = 4
PAGE_SIZE = 128
D_RNN = D_MODEL
RNN_BLOCKS = 16
RNN_BW = D_RNN // RNN_BLOCKS
CONV_W = 4
LRU_C = 8.0
N_HEADS = 32
HEAD_DIM = D_MODEL // N_HEADS
N_KV = 8
IDX_HEADS = 16
IDX_DIM = 64
TOPK_MAX = 256
Q_BLOCK = 128
N_BUCKETS = 32
MAX_DIST = 128
N_EXPERTS = 64
TOP_K = 8
N_GROUPS = 8
TOPK_GROUPS = 4
D_EXPERT = 1024
ROUTED_SCALE = 2.5
DN_ALPHA = 2.0 ** 0.25
LN_EPS = 1e-5
SPLITS = (D_RNN, D_RNN, N_HEADS * HEAD_DIM, N_KV * HEAD_DIM, N_KV * HEAD_DIM,
          IDX_HEADS * IDX_DIM, IDX_DIM, IDX_HEADS, D_MODEL, D_MODEL)

F32 = jnp.float32
BF16 = jnp.bfloat16

N_TOK = SEQ + DEC_BATCH * DEC_SEQ
MM_TM = 640
MM_TN = 512
MOE_MB = 512
MOE_NB = -(-(N_TOK * TOP_K) // MOE_MB) + N_EXPERTS
MOE_CE = 256
VMEM_LIMIT = 48 * 1024 * 1024


def _mm_kernel(x_ref, w_ref, o_ref, wbf_ref):
    @pl.when(pl.program_id(1) == 0)
    def _():
        wbf_ref[...] = w_ref[...].astype(BF16)

    o_ref[...] = jnp.dot(x_ref[...], wbf_ref[...], preferred_element_type=F32).astype(o_ref.dtype)


def _mm(x, w, col0=0, ncols=None, out_dtype=F32, tm=MM_TM, tn=MM_TN):
    m, k = x.shape
    ncols = w.shape[1] - col0 if ncols is None else ncols
    tn = min(tn, ncols)
    assert m % tm == 0 and ncols % tn == 0 and col0 % tn == 0
    cb = col0 // tn
    return pl.pallas_call(
        _mm_kernel,
        grid=(ncols // tn, m // tm),
        in_specs=[pl.BlockSpec((tm, k), lambda j, i: (i, 0)),
                  pl.BlockSpec((k, tn), lambda j, i: (0, j + cb))],
        out_specs=pl.BlockSpec((tm, tn), lambda j, i: (i, j)),
        out_shape=jax.ShapeDtypeStruct((m, ncols), out_dtype),
        scratch_shapes=[pltpu.VMEM((k, tn), BF16)],
        compiler_params=pltpu.CompilerParams(dimension_semantics=("arbitrary", "arbitrary"),
                                             vmem_limit_bytes=VMEM_LIMIT),
    )(x, w)


def _mm_f32_kernel(x_ref, w_ref, o_ref):
    o_ref[...] = jnp.dot(x_ref[...], w_ref[...], preferred_element_type=F32,
                         precision=lax.Precision.HIGHEST)


def _mm_f32(x, w, tm=MM_TM):
    m, k = x.shape
    n = w.shape[1]
    return pl.pallas_call(
        _mm_f32_kernel,
        grid=(m // tm,),
        in_specs=[pl.BlockSpec((tm, k), lambda i: (i, 0)),
                  pl.BlockSpec((k, n), lambda i: (0, 0))],
        out_specs=pl.BlockSpec((tm, n), lambda i: (i, 0)),
        out_shape=jax.ShapeDtypeStruct((m, n), F32),
        compiler_params=pltpu.CompilerParams(dimension_semantics=("arbitrary",),
                                             vmem_limit_bytes=VMEM_LIMIT),
    )(x, w)


def _expert_changed(be_ref, b):
    return (b == 0) | (be_ref[b] != be_ref[jnp.maximum(b - 1, 0)])


def _expert_up_kernel(be_ref, nu_ref, x_ref, w1_ref, w3_ref, h_ref, w1bf_ref, w3bf_ref):
    b = pl.program_id(1)

    @pl.when(b < nu_ref[0])
    def _():
        @pl.when(_expert_changed(be_ref, b))
        def _():
            w1bf_ref[...] = w1_ref[0].astype(BF16)
            w3bf_ref[...] = w3_ref[0].astype(BF16)

        x = x_ref[...]
        a = jnp.dot(x, w1bf_ref[...], preferred_element_type=F32)
        g = jnp.dot(x, w3bf_ref[...], preferred_element_type=F32)
        h_ref[...] = (a * jax.nn.sigmoid(a) * g).astype(h_ref.dtype)


def _expert_down_kernel(be_ref, nu_ref, h_ref, w2_ref, sw_ref, y_ref, w2bf_ref):
    b = pl.program_id(1)

    @pl.when(b < nu_ref[0])
    def _():
        @pl.when(_expert_changed(be_ref, b))
        def _():
            w2bf_ref[...] = w2_ref[0].astype(BF16)

        y = jnp.dot(h_ref[...], w2bf_ref[...], preferred_element_type=F32)
        y_ref[...] = (y * sw_ref[...]).astype(y_ref.dtype)


def _experts(xs, slot_w, blk_e, n_used, w_e1, w_e3, w_e2):
    d = xs.shape[1]
    nce = D_EXPERT // MOE_CE
    row = lambda c, b, be, nu: jnp.minimum(b, nu[0] - 1)
    h = pl.pallas_call(
        _expert_up_kernel,
        grid_spec=pltpu.PrefetchScalarGridSpec(
            num_scalar_prefetch=2,
            grid=(nce, MOE_NB),
            in_specs=[pl.BlockSpec((MOE_MB, d), lambda *a: (row(*a), 0)),
                      pl.BlockSpec((1, d, MOE_CE), lambda *a: (a[2][row(*a)], 0, a[0])),
                      pl.BlockSpec((1, d, MOE_CE), lambda *a: (a[2][row(*a)], 0, a[0]))],
            out_specs=pl.BlockSpec((MOE_MB, MOE_CE), lambda *a: (row(*a), a[0])),
            scratch_shapes=[pltpu.VMEM((d, MOE_CE), BF16), pltpu.VMEM((d, MOE_CE), BF16)]),
        out_shape=jax.ShapeDtypeStruct((MOE_NB * MOE_MB, D_EXPERT), BF16),
        compiler_params=pltpu.CompilerParams(dimension_semantics=("arbitrary", "arbitrary"),
                                             vmem_limit_bytes=VMEM_LIMIT),
    )(blk_e, n_used, xs, w_e1, w_e3)
    tn = 1024
    return pl.pallas_call(
        _expert_down_kernel,
        grid_spec=pltpu.PrefetchScalarGridSpec(
            num_scalar_prefetch=2,
            grid=(d // tn, MOE_NB),
            in_specs=[pl.BlockSpec((MOE_MB, D_EXPERT), lambda *a: (row(*a), 0)),
                      pl.BlockSpec((1, D_EXPERT, tn), lambda *a: (a[2][row(*a)], 0, a[0])),
                      pl.BlockSpec((MOE_MB, 1), lambda *a: (row(*a), 0))],
            out_specs=pl.BlockSpec((MOE_MB, tn), lambda *a: (row(*a), a[0])),
            scratch_shapes=[pltpu.VMEM((D_EXPERT, tn), BF16)]),
        out_shape=jax.ShapeDtypeStruct((MOE_NB * MOE_MB, d), BF16),
        compiler_params=pltpu.CompilerParams(dimension_semantics=("arbitrary", "arbitrary"),
                                             vmem_limit_bytes=VMEM_LIMIT),
    )(blk_e, n_used, h, w_e2, slot_w)


LANES = 128


def _router_kernel(x_ref, w_ref, b_ref, eidx_ref, gw_ref, hit_ref):
    gsz = N_EXPERTS // N_GROUPS
    logits = jnp.dot(x_ref[...], w_ref[...], preferred_element_type=F32, precision=lax.Precision.HIGHEST)
    s = jax.nn.sigmoid(logits)
    sb = s + b_ref[...]
    lane = lax.broadcasted_iota(jnp.int32, sb.shape, 1)

    def group_reduce(v, op):
        step = 1
        while step < gsz:
            partner = jnp.where((lane & step) != 0, pltpu.roll(v, step, axis=1),
                                pltpu.roll(v, LANES - step, axis=1))
            v = op(v, partner)
            step *= 2
        return v

    max1 = group_reduce(sb, jnp.maximum)
    first = group_reduce(jnp.where(sb == max1, lane, LANES), jnp.minimum)
    max2 = group_reduce(jnp.where(lane == first, -jnp.inf, sb), jnp.maximum)
    gscore = max1 + max2
    gi = (lane % N_EXPERTS) // gsz
    rank = jnp.zeros(sb.shape, jnp.int32)
    for k in range(1, N_GROUPS):
        other = pltpu.roll(gscore, k * gsz, axis=1)
        oi = (gi - k) % N_GROUPS
        rank = rank + jnp.where((other > gscore) | ((other == gscore) & (oi < gi)), 1, 0)
    cand = jnp.where((rank < TOPK_GROUPS) & (lane < N_EXPERTS), sb, -jnp.inf)
    eidx = jnp.zeros(sb.shape, jnp.int32)
    gw = jnp.zeros(sb.shape, F32)
    hit = jnp.zeros(sb.shape, jnp.int32)
    for r in range(TOP_K):
        best = jnp.max(cand, axis=1, keepdims=True)
        idx = jnp.min(jnp.where(cand == best, lane, LANES), axis=1, keepdims=True)
        chosen = lane == idx
        val = jnp.sum(jnp.where(chosen, s, 0.0), axis=1, keepdims=True)
        eidx = jnp.where(lane == r, idx, eidx)
        gw = jnp.where(lane == r, val, gw)
        hit = jnp.where(chosen, 1, hit)
        cand = jnp.where(chosen, -jnp.inf, cand)
    eidx_ref[...] = eidx
    gw_ref[...] = gw / jnp.sum(gw, axis=1, keepdims=True) * ROUTED_SCALE
    hit_ref[...] = hit


def _router(x, w_router, router_bias, tm=MM_TM):
    n, d = x.shape
    out_spec = pl.BlockSpec((tm, LANES), lambda i: (i, 0))
    eidx, gw, hit = pl.pallas_call(
        _router_kernel,
        grid=(n // tm,),
        in_specs=[pl.BlockSpec((tm, d), lambda i: (i, 0)),
                  pl.BlockSpec((d, LANES), lambda i: (0, 0)),
                  pl.BlockSpec((1, LANES), lambda i: (0, 0))],
        out_specs=[out_spec, out_spec, out_spec],
        out_shape=[jax.ShapeDtypeStruct((n, LANES), jnp.int32), jax.ShapeDtypeStruct((n, LANES), F32),
                   jax.ShapeDtypeStruct((n, LANES), jnp.int32)],
        compiler_params=pltpu.CompilerParams(dimension_semantics=("arbitrary",),
                                             vmem_limit_bytes=VMEM_LIMIT),
    )(x, jnp.concatenate([w_router, w_router], axis=1),
      jnp.concatenate([router_bias, router_bias]).reshape(1, LANES))
    return eidx[:, :TOP_K], gw[:, :TOP_K], hit[:, :N_EXPERTS]


IDX_QB = 128
ATT_TK = 256
ATT_QB = 256
NEG = -1e30
I32_MIN = -2 ** 31
_NT = (((1,), (1,)), ((), ()))


def _index_select_kernel(qi_ref, wi_ref, ki_ref, mask_ref, skey_ref, wb_ref, qh_ref):
    b = pl.program_id(0)
    n_tiles = mask_ref.shape[1]
    nt = (b * IDX_QB + IDX_QB + ATT_TK - 1) // ATT_TK
    qi = (qi_ref[...] * IDX_DIM ** -0.5).astype(BF16)
    w = wi_ref[...] * IDX_HEADS ** -0.5
    for h in range(IDX_HEADS):
        qh_ref[h] = qi[:, h * IDX_DIM:(h + 1) * IDX_DIM]
        wb_ref[h] = jnp.broadcast_to(w[:, h:h + 1], (IDX_QB, ATT_TK))
    row = b * IDX_QB + lax.broadcasted_iota(jnp.int32, (IDX_QB, ATT_TK), 0)
    col = lax.broadcasted_iota(jnp.int32, (IDX_QB, ATT_TK), 1)

    def score_tile(j, carry):
        kt = ki_ref[pl.ds(pl.multiple_of(j * ATT_TK, ATT_TK), ATT_TK), :]
        acc = jnp.zeros((IDX_QB, ATT_TK), F32)
        for h in range(IDX_HEADS):
            s = lax.dot_general(qh_ref[h], kt, _NT, preferred_element_type=F32)
            acc = acc + wb_ref[h] * jnp.maximum(s, 0.0)
        bits = pltpu.bitcast(acc, jnp.int32)
        key = bits ^ ((bits >> 31) & 0x7FFFFFFF)
        skey_ref[j] = jnp.where(col + j * ATT_TK <= row, key, I32_MIN)
        return carry

    lax.fori_loop(0, nt, score_tile, 0)

    half = ATT_TK // 2

    def bit_body(i, ans):
        cand_u = ans | lax.shift_left(jnp.int32(1), 31 - i)
        cand = jnp.broadcast_to(cand_u ^ I32_MIN, (IDX_QB, half))

        def count(j, c):
            t = skey_ref[j]
            return (c + jnp.where(t[:, :half] >= cand, 1, 0) + jnp.where(t[:, half:] >= cand, 1, 0))

        c = lax.fori_loop(0, nt, count, jnp.zeros((IDX_QB, half), jnp.int32))
        cnt = jnp.sum(c, axis=1, keepdims=True)
        return jnp.where(cnt >= TOPK_MAX, cand_u, ans)

    ans = lax.fori_loop(0, 32, bit_body, jnp.zeros((IDX_QB, 1), jnp.int32))
    thr = jnp.broadcast_to(jnp.maximum(ans ^ I32_MIN, I32_MIN + 1), (IDX_QB, ATT_TK))

    def write_tile(j, carry):
        mask_ref[0, j] = jnp.where(skey_ref[j] >= thr, 0.0, NEG).astype(BF16)
        return carry

    def write_empty(j, carry):
        mask_ref[0, j] = jnp.full((IDX_QB, ATT_TK), NEG, BF16)
        return carry

    lax.fori_loop(0, nt, write_tile, 0)
    lax.fori_loop(nt, n_tiles, write_empty, 0)


def _index_select(qi, ki, wi):
    s = qi.shape[0]
    n_tiles = s // ATT_TK
    return pl.pallas_call(
        _index_select_kernel,
        grid=(s // IDX_QB,),
        in_specs=[pl.BlockSpec((IDX_QB, IDX_HEADS * IDX_DIM), lambda b: (b, 0)),
                  pl.BlockSpec((IDX_QB, IDX_HEADS), lambda b: (b, 0)),
                  pl.BlockSpec((s, IDX_DIM), lambda b: (0, 0))],
        out_specs=pl.BlockSpec((1, n_tiles, IDX_QB, ATT_TK), lambda b: (b, 0, 0, 0)),
        out_shape=jax.ShapeDtypeStruct((s // IDX_QB, n_tiles, IDX_QB, ATT_TK), BF16),
        scratch_shapes=[pltpu.VMEM((n_tiles, IDX_QB, ATT_TK), jnp.int32),
                        pltpu.VMEM((IDX_HEADS, IDX_QB, ATT_TK), F32),
                        pltpu.VMEM((IDX_HEADS, IDX_QB, IDX_DIM), BF16)],
        compiler_params=pltpu.CompilerParams(dimension_semantics=("arbitrary",),
                                             vmem_limit_bytes=VMEM_LIMIT),
    )(qi, wi, ki.astype(BF16))


def _masked_attn_kernel(q_ref, k_ref, v_ref, mask_ref, bd_ref, bo_ref, o_ref,
                        qs_ref, m_ref, l_ref, acc_ref):
    grp = N_HEADS // N_KV
    qb = pl.program_id(1)
    q = (q_ref[...] * HEAD_DIM ** -0.5).astype(BF16)
    for hh in range(grp):
        qs_ref[hh] = q[:, hh * HEAD_DIM:(hh + 1) * HEAD_DIM]
    m_ref[...] = jnp.full(m_ref.shape, NEG, F32)
    l_ref[...] = jnp.zeros(l_ref.shape, F32)
    acc_ref[...] = jnp.zeros(acc_ref.shape, F32)

    def tile(j, bias_ref):
        off = pl.multiple_of(j * ATT_TK, ATT_TK)
        kt = k_ref[pl.ds(off, ATT_TK), :]
        vt = v_ref[pl.ds(off, ATT_TK), :]
        for r in range(ATT_QB // IDX_QB):
            rows = slice(r * IDX_QB, (r + 1) * IDX_QB)
            mk = mask_ref[r, j].astype(F32)
            for hh in range(grp):
                s = lax.dot_general(qs_ref[hh, rows], kt, _NT, preferred_element_type=F32) + mk
                if bias_ref is not None:
                    s = s + bias_ref[hh, rows]
                m_old = m_ref[hh, rows]
                m_new = jnp.maximum(m_old, jnp.max(s, axis=1, keepdims=True))
                alpha = jnp.exp(m_old - m_new)
                p = jnp.exp(s - jnp.concatenate([m_new] * (ATT_TK // HEAD_DIM), axis=1))
                l_ref[hh, rows] = alpha * l_ref[hh, rows] + jnp.sum(p, axis=1, keepdims=True)
                acc_ref[hh, rows] = alpha * acc_ref[hh, rows] + jnp.dot(
                    p.astype(BF16), vt, preferred_element_type=F32)
                m_ref[hh, rows] = m_new

    def far_tile(j, carry):
        tile(j, None)
        return carry

    lax.fori_loop(0, jnp.maximum(qb - 1, 0), far_tile, 0)

    @pl.when(qb >= 1)
    def _():
        tile(qb - 1, bo_ref)

    tile(qb, bd_ref)
    for hh in range(grp):
        o_ref[:, hh * HEAD_DIM:(hh + 1) * HEAD_DIM] = (acc_ref[hh] / l_ref[hh]).astype(o_ref.dtype)


def _rel_bias_tiles(rel_bias):
    a = jnp.arange(ATT_QB, dtype=jnp.int32)[:, None]
    c = jnp.arange(ATT_TK, dtype=jnp.int32)[None, :]
    far = rel_bias[N_BUCKETS - 1]
    diag = rel_bias[_rel_bucket(a - c)] - far
    prev = rel_bias[_rel_bucket(a - c + ATT_TK)] - far
    return diag.transpose(2, 0, 1), prev.transpose(2, 0, 1)


def _masked_attn(q, k, v, mask, rel_bias):
    s = q.shape[0]
    grp = N_HEADS // N_KV
    n_tiles = s // ATT_TK
    bd, bo = _rel_bias_tiles(rel_bias)
    wq = grp * HEAD_DIM
    return pl.pallas_call(
        _masked_attn_kernel,
        grid=(N_KV, s // ATT_QB),
        in_specs=[pl.BlockSpec((ATT_QB, wq), lambda g, i: (i, g)),
                  pl.BlockSpec((s, HEAD_DIM), lambda g, i: (0, g)),
                  pl.BlockSpec((s, HEAD_DIM), lambda g, i: (0, g)),
                  pl.BlockSpec((ATT_QB // IDX_QB, n_tiles, IDX_QB, ATT_TK), lambda g, i: (i, 0, 0, 0)),
                  pl.BlockSpec((grp, ATT_QB, ATT_TK), lambda g, i: (g, 0, 0)),
                  pl.BlockSpec((grp, ATT_QB, ATT_TK), lambda g, i: (g, 0, 0))],
        out_specs=pl.BlockSpec((ATT_QB, wq), lambda g, i: (i, g)),
        out_shape=jax.ShapeDtypeStruct((s, N_HEADS * HEAD_DIM), BF16),
        scratch_shapes=[pltpu.VMEM((grp, ATT_QB, HEAD_DIM), BF16),
                        pltpu.VMEM((grp, ATT_QB, HEAD_DIM), F32),
                        pltpu.VMEM((grp, ATT_QB, HEAD_DIM), F32),
                        pltpu.VMEM((grp, ATT_QB, HEAD_DIM), F32)],
        compiler_params=pltpu.CompilerParams(dimension_semantics=("arbitrary", "arbitrary"),
                                             vmem_limit_bytes=VMEM_LIMIT),
    )(q, k.astype(BF16), v.astype(BF16), mask, bd, bo)


SMP_PG = 8
SMP_TP = 8


def _order_key(score):
    bits = pltpu.bitcast(score, jnp.int32)
    return bits ^ ((bits >> 31) & 0x7FFFFFFF)


def _sample_select_kernel(pt_ref, *refs):
    pages = refs[:SMP_PG]
    kn_ref, qi_ref, w_ref, skey_ref, thr_ref = refs[SMP_PG:]
    p = pl.program_id(1)
    n_tiles = skey_ref.shape[1]
    qi = qi_ref[0]
    w = w_ref[0].reshape(IDX_HEADS, SMP_TP, PAGE_SIZE)

    def score(kt):
        s = lax.dot_general(qi, kt.astype(BF16), _NT, preferred_element_type=F32)
        s = jnp.maximum(s, 0.0).reshape(IDX_HEADS, SMP_TP, PAGE_SIZE)
        return _order_key(jnp.sum(w * s, axis=0))

    for i in range(SMP_PG):
        skey_ref[0, p * SMP_PG + i] = score(pages[i][0, 0])

    @pl.when(p == pl.num_programs(1) - 1)
    def _():
        t = lax.broadcasted_iota(jnp.int32, (SMP_TP, PAGE_SIZE), 0)
        j = lax.broadcasted_iota(jnp.int32, (SMP_TP, PAGE_SIZE), 1)
        skey_ref[0, n_tiles - 1] = jnp.where((j <= t) & (j < DEC_SEQ), score(kn_ref[0]), I32_MIN)

        def bit_body(i, ans):
            cand_u = ans | lax.shift_left(jnp.int32(1), 31 - i)
            cand = jnp.broadcast_to(cand_u ^ I32_MIN, (SMP_TP, PAGE_SIZE))

            def count(jt, c):
                return c + jnp.where(skey_ref[0, jt] >= cand, 1, 0)

            c = lax.fori_loop(0, n_tiles, count, jnp.zeros((SMP_TP, PAGE_SIZE), jnp.int32))
            return jnp.where(jnp.sum(c, axis=1, keepdims=True) >= TOPK_MAX, cand_u, ans)

        ans = lax.fori_loop(0, 32, bit_body, jnp.zeros((SMP_TP, 1), jnp.int32))
        thr_ref[0] = jnp.broadcast_to(jnp.maximum(ans ^ I32_MIN, I32_MIN + 1), (SMP_TP, PAGE_SIZE))


def _sample_attn_kernel(pt_ref, *refs):
    kp, vp = refs[:SMP_PG], refs[SMP_PG:2 * SMP_PG]
    kn_ref, vn_ref, q_ref, skey_ref, thr_ref, bl_ref, bn_ref, o_ref, m_ref, l_ref, acc_ref = refs[2 * SMP_PG:]
    grp = N_HEADS // N_KV
    p = pl.program_id(1)
    last = p == pl.num_programs(1) - 1
    n_tiles = skey_ref.shape[1]

    @pl.when(p == 0)
    def _():
        m_ref[...] = jnp.full(m_ref.shape, NEG, F32)
        l_ref[...] = jnp.zeros(l_ref.shape, F32)
        acc_ref[...] = jnp.zeros(acc_ref.shape, F32)

    thr = thr_ref[0]

    def mask_of(tile):
        mk = jnp.where(skey_ref[0, tile] >= thr, 0.0, NEG)
        return jnp.concatenate([mk] * grp, axis=0)

    def update(g, k_g, v_g, add):
        s = lax.dot_general(q_ref[0, g], k_g, _NT, preferred_element_type=F32) + add
        m_old = m_ref[g]
        m_new = jnp.maximum(m_old, jnp.max(s, axis=1, keepdims=True))
        alpha = jnp.exp(m_old - m_new)
        pr = jnp.exp(s - jnp.concatenate([m_new] * (s.shape[1] // HEAD_DIM), axis=1))
        l_ref[g] = alpha * l_ref[g] + jnp.sum(pr, axis=1, keepdims=True)
        acc_ref[g] = alpha * acc_ref[g] + jnp.dot(pr.astype(BF16), v_g, preferred_element_type=F32)
        m_ref[g] = m_new

    mk = jnp.concatenate([mask_of(p * SMP_PG + i) for i in range(SMP_PG)], axis=1)
    near = jnp.where(last, 1.0, 0.0)
    for g in range(N_KV):
        k_g = jnp.concatenate([kp[i][0, 0, :, g, :].astype(BF16) for i in range(SMP_PG)], axis=0)
        v_g = jnp.concatenate([vp[i][0, 0, :, g, :].astype(BF16) for i in range(SMP_PG)], axis=0)
        bias = jnp.concatenate([jnp.zeros((grp * SMP_TP, (SMP_PG - 1) * PAGE_SIZE), F32),
                                near * bl_ref[g]], axis=1)
        update(g, k_g, v_g, mk + bias)

    @pl.when(last)
    def _():
        mn = mask_of(n_tiles - 1)
        for g in range(N_KV):
            cols = slice(g * HEAD_DIM, (g + 1) * HEAD_DIM)
            update(g, kn_ref[0, :, cols].astype(BF16), vn_ref[0, :, cols].astype(BF16), mn + bn_ref[g])
        for g in range(N_KV):
            o_ref[0, g] = acc_ref[g] / l_ref[g]


def _sample_attn_pallas(q, k, v, qi, ki, wi, cache_k, cache_v, cache_idx_k, page_table, rel_bias):
    bsz, nq = q.shape[:2]
    n_pages = page_table.shape[1]
    past = n_pages * PAGE_SIZE
    grp = N_HEADS // N_KV
    n_steps = n_pages // SMP_PG
    tpad = ((0, 0), (0, 0), (0, SMP_TP - nq), (0, 0))
    kpad = ((0, 0), (0, PAGE_SIZE - nq), (0, 0))
    qi_r = jnp.pad((qi * IDX_DIM ** -0.5).reshape(bsz, nq, IDX_HEADS, IDX_DIM).transpose(0, 2, 1, 3), tpad)
    qi_r = qi_r.reshape(bsz, IDX_HEADS * SMP_TP, IDX_DIM).astype(BF16)
    w_r = jnp.pad((wi * IDX_HEADS ** -0.5).transpose(0, 2, 1), ((0, 0), (0, 0), (0, SMP_TP - nq)))
    w_r = jnp.broadcast_to(w_r.reshape(bsz, IDX_HEADS * SMP_TP, 1), (bsz, IDX_HEADS * SMP_TP, PAGE_SIZE))
    page_specs = lambda *tail: [
        pl.BlockSpec((1, 1, PAGE_SIZE) + tail, functools.partial(
            lambda b, p, pt, i: (0, pt[b, p * SMP_PG + i], 0) + (0,) * len(tail), i=i))
        for i in range(SMP_PG)]
    per_batch = lambda *shape: pl.BlockSpec((1,) + shape, lambda b, p, pt: (b,) + (0,) * len(shape))
    skey, thr = pl.pallas_call(
        _sample_select_kernel,
        grid_spec=pltpu.PrefetchScalarGridSpec(
            num_scalar_prefetch=1,
            grid=(bsz, n_steps),
            in_specs=page_specs(IDX_DIM) + [per_batch(PAGE_SIZE, IDX_DIM),
                                            per_batch(IDX_HEADS * SMP_TP, IDX_DIM),
                                            per_batch(IDX_HEADS * SMP_TP, PAGE_SIZE)],
            out_specs=[per_batch(n_pages + 1, SMP_TP, PAGE_SIZE), per_batch(SMP_TP, PAGE_SIZE)]),
        out_shape=[jax.ShapeDtypeStruct((bsz, n_pages + 1, SMP_TP, PAGE_SIZE), jnp.int32),
                   jax.ShapeDtypeStruct((bsz, SMP_TP, PAGE_SIZE), jnp.int32)],
        compiler_params=pltpu.CompilerParams(dimension_semantics=("arbitrary", "arbitrary"),
                                             vmem_limit_bytes=VMEM_LIMIT),
    )(page_table, *([cache_idx_k] * SMP_PG), jnp.pad(ki, kpad), qi_r, w_r)

    q_r = jnp.pad((q * HEAD_DIM ** -0.5).reshape(bsz, nq, N_KV, grp, HEAD_DIM).transpose(0, 2, 3, 1, 4),
                  ((0, 0), (0, 0), (0, 0), (0, SMP_TP - nq), (0, 0)))
    q_r = q_r.reshape(bsz, N_KV, grp * SMP_TP, HEAD_DIM).astype(BF16)
    t = jnp.arange(SMP_TP, dtype=jnp.int32)[:, None]
    lane = jnp.arange(PAGE_SIZE, dtype=jnp.int32)[None, :]
    rows = lambda b: b.transpose(2, 0, 1).reshape(N_KV, grp * SMP_TP, PAGE_SIZE)
    far = rel_bias[N_BUCKETS - 1]
    b_last = rows(rel_bias[_rel_bucket(PAGE_SIZE + t - lane)] - far)
    b_new = rows(rel_bias[_rel_bucket(t - lane)] - far)
    kvw = N_KV * HEAD_DIM
    const = lambda *shape: pl.BlockSpec(shape, lambda b, p, pt: (0,) * len(shape))
    o = pl.pallas_call(
        _sample_attn_kernel,
        grid_spec=pltpu.PrefetchScalarGridSpec(
            num_scalar_prefetch=1,
            grid=(bsz, n_steps),
            in_specs=page_specs(N_KV, HEAD_DIM) + page_specs(N_KV, HEAD_DIM) + [
                per_batch(PAGE_SIZE, kvw), per_batch(PAGE_SIZE, kvw),
                per_batch(N_KV, grp * SMP_TP, HEAD_DIM),
                per_batch(n_pages + 1, SMP_TP, PAGE_SIZE), per_batch(SMP_TP, PAGE_SIZE),
                const(N_KV, grp * SMP_TP, PAGE_SIZE), const(N_KV, grp * SMP_TP, PAGE_SIZE)],
            out_specs=per_batch(N_KV, grp * SMP_TP, HEAD_DIM),
            scratch_shapes=[pltpu.VMEM((N_KV, grp * SMP_TP, HEAD_DIM), F32)] * 3),
        out_shape=jax.ShapeDtypeStruct((bsz, N_KV, grp * SMP_TP, HEAD_DIM), F32),
        compiler_params=pltpu.CompilerParams(dimension_semantics=("arbitrary", "arbitrary"),
                                             vmem_limit_bytes=VMEM_LIMIT),
    )(page_table, *([cache_k] * SMP_PG), *([cache_v] * SMP_PG),
      jnp.pad(k, kpad), jnp.pad(v, kpad), q_r, skey, thr, b_last, b_new)
    o = o.reshape(bsz, N_KV, grp, SMP_TP, HEAD_DIM)[:, :, :, :nq]
    return o.transpose(0, 3, 1, 2, 4).reshape(bsz, nq, N_HEADS * HEAD_DIM)


LRU_TT = 512
_GELU_C = math.sqrt(2.0 / math.pi)


def _gelu_tanh(y):
    return 0.5 * y * (1.0 + jnp.tanh(_GELU_C * (y + 0.044715 * (y * y * y))))


def _lru_coeffs(xc, wa_ref, ba_ref, wi_ref, bi_ref, lam_ref):
    xb = xc.astype(BF16)
    r = jax.nn.sigmoid(jnp.dot(xb, wa_ref[0].astype(BF16), preferred_element_type=F32) + ba_ref[0])
    i = jax.nn.sigmoid(jnp.dot(xb, wi_ref[0].astype(BF16), preferred_element_type=F32) + bi_ref[0])
    z = -lam_ref[...]
    softplus = jnp.maximum(z, 0.0) + jnp.log1p(jnp.exp(-jnp.abs(z)))
    log_a = -LRU_C * r * softplus
    t = jnp.tanh(log_a)
    u = jnp.sqrt(-2.0 * t / (1.0 - t)) * (i * xc)
    return jnp.exp(log_a), u


def _rglru_prompt_kernel(x_ref, y_ref, cw_ref, cb_ref, wa_ref, ba_ref, wi_ref, bi_ref, lam_ref,
                         o_ref, nb_ref, hl_ref, xbuf_ref, hc_ref):
    c = pl.program_id(1)
    tt = x_ref.shape[0]

    @pl.when(c == 0)
    def _():
        xbuf_ref[0:8] = jnp.zeros((8, RNN_BW), F32)
        hc_ref[...] = jnp.zeros(hc_ref.shape, F32)

    x = x_ref[...]
    xbuf_ref[8:8 + tt] = x
    xc = cb_ref[...] + cw_ref[CONV_W - 1:CONV_W] * x
    for j in range(CONV_W - 1):
        xc = xc + cw_ref[j:j + 1] * xbuf_ref[8 - (CONV_W - 1) + j:8 - (CONV_W - 1) + j + tt]
    a, u = _lru_coeffs(xc, wa_ref, ba_ref, wi_ref, bi_ref, lam_ref)
    row = lax.broadcasted_iota(jnp.int32, (tt, RNN_BW), 0)
    s = 1
    while s < tt:
        keep = row >= s
        a_prev = jnp.where(keep, pltpu.roll(a, s, axis=0), 1.0)
        u_prev = jnp.where(keep, pltpu.roll(u, s, axis=0), 0.0)
        u = a * u_prev + u
        a = a * a_prev
        s *= 2
    h = a * hc_ref[0:1] + u
    o_ref[...] = (h * _gelu_tanh(y_ref[...])).astype(o_ref.dtype)
    hc_ref[0:1] = h[tt - 1:tt]
    xbuf_ref[0:8] = x[tt - 8:tt]

    @pl.when(c == pl.num_programs(1) - 1)
    def _():
        hl_ref[...] = h[tt - 1:tt]
        nb_ref[...] = x[tt - (CONV_W - 1):tt]


def _rglru_prompt(xr, yr, s, conv_w, conv_b, w_rg_a, b_rg_a, w_rg_i, b_rg_i, lru_lambda):
    vec = lambda t: t.reshape(1, D_RNN)
    blk = lambda t: t.reshape(RNN_BLOCKS, 1, RNN_BW)
    row_spec = pl.BlockSpec((LRU_TT, RNN_BW), lambda n, c: (c, n))
    vec_spec = pl.BlockSpec((1, RNN_BW), lambda n, c: (0, n))
    w_spec = pl.BlockSpec((1, RNN_BW, RNN_BW), lambda n, c: (n, 0, 0))
    b_spec = pl.BlockSpec((1, 1, RNN_BW), lambda n, c: (n, 0, 0))
    return pl.pallas_call(
        _rglru_prompt_kernel,
        grid=(RNN_BLOCKS, s // LRU_TT),
        in_specs=[row_spec, row_spec, pl.BlockSpec((CONV_W, RNN_BW), lambda n, c: (0, n)), vec_spec,
                  w_spec, b_spec, w_spec, b_spec, vec_spec],
        out_specs=[row_spec, pl.BlockSpec((CONV_W - 1, RNN_BW), lambda n, c: (0, n)), vec_spec],
        out_shape=[jax.ShapeDtypeStruct((s, D_RNN), BF16),
                   jax.ShapeDtypeStruct((CONV_W - 1, D_RNN), F32),
                   jax.ShapeDtypeStruct((1, D_RNN), F32)],
        scratch_shapes=[pltpu.VMEM((8 + LRU_TT, RNN_BW), F32), pltpu.VMEM((8, RNN_BW), F32)],
        compiler_params=pltpu.CompilerParams(dimension_semantics=("arbitrary", "arbitrary"),
                                             vmem_limit_bytes=VMEM_LIMIT),
    )(xr, yr, conv_w, vec(conv_b), w_rg_a, blk(b_rg_a), w_rg_i, blk(b_rg_i), vec(lru_lambda))


def _rglru_sample_kernel(x_ref, y_ref, buf_ref, h0_ref, cw_ref, cb_ref, wa_ref, ba_ref, wi_ref, bi_ref,
                         lam_ref, o_ref, nb_ref, hl_ref):
    nt, nb = x_ref.shape[0], x_ref.shape[1]
    xs = [buf_ref[j] for j in range(CONV_W - 1)] + [x_ref[t] for t in range(nt)]
    xc = []
    for t in range(nt):
        acc = cb_ref[...] + cw_ref[0:1] * xs[t]
        for j in range(1, CONV_W):
            acc = acc + cw_ref[j:j + 1] * xs[t + j]
        xc.append(acc)
    a, u = _lru_coeffs(jnp.concatenate(xc, axis=0), wa_ref, ba_ref, wi_ref, bi_ref, lam_ref)
    h = h0_ref[...]
    for t in range(nt):
        h = a[t * nb:(t + 1) * nb] * h + u[t * nb:(t + 1) * nb]
        o_ref[t] = (h * _gelu_tanh(y_ref[t])).astype(o_ref.dtype)
    hl_ref[...] = h
    for j in range(CONV_W - 1):
        nb_ref[j] = xs[nt + j]


def _rglru_sample(xr, yr, buf, h0, conv_w, conv_b, w_rg_a, b_rg_a, w_rg_i, b_rg_i, lru_lambda):
    nt, nb, _ = xr.shape
    vec = lambda t: t.reshape(1, D_RNN)
    blk = lambda t: t.reshape(RNN_BLOCKS, 1, RNN_BW)
    seq_spec = lambda n_rows: pl.BlockSpec((n_rows, nb, RNN_BW), lambda n: (0, 0, n))
    vec_spec = pl.BlockSpec((1, RNN_BW), lambda n: (0, n))
    w_spec = pl.BlockSpec((1, RNN_BW, RNN_BW), lambda n: (n, 0, 0))
    b_spec = pl.BlockSpec((1, 1, RNN_BW), lambda n: (n, 0, 0))
    h_spec = pl.BlockSpec((nb, RNN_BW), lambda n: (0, n))
    return pl.pallas_call(
        _rglru_sample_kernel,
        grid=(RNN_BLOCKS,),
        in_specs=[seq_spec(nt), seq_spec(nt), seq_spec(CONV_W - 1), h_spec,
                  pl.BlockSpec((CONV_W, RNN_BW), lambda n: (0, n)), vec_spec,
                  w_spec, b_spec, w_spec, b_spec, vec_spec],
        out_specs=[seq_spec(nt), seq_spec(CONV_W - 1), h_spec],
        out_shape=[jax.ShapeDtypeStruct((nt, nb, D_RNN), BF16),
                   jax.ShapeDtypeStruct((CONV_W - 1, nb, D_RNN), F32),
                   jax.ShapeDtypeStruct((nb, D_RNN), F32)],
        compiler_params=pltpu.CompilerParams(dimension_semantics=("arbitrary",),
                                             vmem_limit_bytes=VMEM_LIMIT),
    )(xr, yr, buf, h0, conv_w, vec(conv_b), w_rg_a, blk(b_rg_a), w_rg_i, blk(b_rg_i), vec(lru_lambda))


def _layernorm(x, g, b):
    mu = jnp.mean(x, axis=-1, keepdims=True)
    var = jnp.mean(jnp.square(x - mu), axis=-1, keepdims=True)
    return (x - mu) * lax.rsqrt(var + LN_EPS) * g + b


def _rel_bucket(dist):
    max_exact = N_BUCKETS // 2
    d = jnp.maximum(dist, 0)
    large = max_exact + (jnp.log(jnp.maximum(d, 1).astype(F32) / max_exact)
                         / math.log(MAX_DIST / max_exact) * (N_BUCKETS - max_exact)).astype(jnp.int32)
    large = jnp.minimum(large, N_BUCKETS - 1)
    return jnp.where(d < max_exact, d, large)


def _causal_conv(x, buf, w, b):
    T = x.shape[1]
    xc = jnp.concatenate([buf, x], axis=1)
    y = b
    for j in range(CONV_W):
        y = y + w[j] * xc[:, j:j + T]
    return y, xc[:, T:]


def _rglru(xr, yr, conv_buf, h0, conv_w, conv_b, w_rg_a, b_rg_a, w_rg_i, b_rg_i, lru_lambda):
    bsz, T, _ = xr.shape
    xf, new_buf = _causal_conv(xr, conv_buf, conv_w, conv_b)
    xb = xf.reshape(bsz, T, RNN_BLOCKS, RNN_BW)
    r = jax.nn.sigmoid(jnp.einsum('btnc,ncd->btnd', xb, w_rg_a) + b_rg_a).reshape(bsz, T, D_RNN)
    i = jax.nn.sigmoid(jnp.einsum('btnc,ncd->btnd', xb, w_rg_i) + b_rg_i).reshape(bsz, T, D_RNN)
    log_a = -LRU_C * r * jax.nn.softplus(-lru_lambda)
    a = jnp.exp(log_a)
    u = jnp.sqrt(-jnp.expm1(2.0 * log_a)) * (i * xf)
    u = u.at[:, 0].add(a[:, 0] * h0)

    def combine(left, right):
        return left[0] * right[0], right[0] * left[1] + right[1]

    _, h = lax.associative_scan(combine, (a, u), axis=1)
    return h * jax.nn.gelu(yr), new_buf, h[:, -1]


def _sparse_attn(q, qi, wi, qpos, kidx, gather_kv, ktop, rel_bias):
    bsz, nq = q.shape[:2]
    L = kidx.shape[1]
    s = jnp.einsum('bqhd,bld->bqhl', qi, kidx) * IDX_DIM ** -0.5
    score = jnp.einsum('bqh,bqhl->bql', wi * IDX_HEADS ** -0.5, jax.nn.relu(s))
    kpos = jnp.arange(L, dtype=jnp.int32)
    score = jnp.where(kpos[None, None, :] <= qpos[None, :, None], score, -jnp.inf)
    _, sel = lax.top_k(score, ktop)
    ks, vs = gather_kv(sel)
    grp = N_HEADS // N_KV
    qg = q.reshape(bsz, nq, N_KV, grp, HEAD_DIM)
    logits = jnp.einsum('bqkgd,bqjkd->bqkgj', qg, ks) * HEAD_DIM ** -0.5
    dist = qpos[None, :, None] - sel
    bias = rel_bias[_rel_bucket(dist)]
    bias = bias.reshape(bsz, nq, ktop, N_KV, grp).transpose(0, 1, 3, 4, 2)
    logits = jnp.where((dist >= 0)[:, :, None, None, :], logits + bias, -jnp.inf)
    p = jax.nn.softmax(logits, axis=-1)
    o = jnp.einsum('bqkgj,bqjkd->bqkgd', p, vs)
    return o.reshape(bsz, nq, N_HEADS * HEAD_DIM)


def _take_rows(a, idx):
    return jax.vmap(lambda t, i: t[i])(a, idx)


def _prompt_attn(q, k, v, qi, ki, wi, rel_bias):
    bsz, S = q.shape[:2]
    ktop = min(TOPK_MAX, S // 4)

    def gather(sel):
        return _take_rows(k, sel), _take_rows(v, sel)

    def block(b):
        q0 = b * Q_BLOCK
        sl = lambda t: lax.dynamic_slice_in_dim(t, q0, Q_BLOCK, axis=1)
        qpos = q0 + jnp.arange(Q_BLOCK, dtype=jnp.int32)
        return _sparse_attn(sl(q), sl(qi), sl(wi), qpos, ki, gather, ktop, rel_bias)

    out = lax.map(block, jnp.arange(S // Q_BLOCK))
    return out.transpose(1, 0, 2, 3).reshape(bsz, S, N_HEADS * HEAD_DIM)


def _sample_attn(q, k, v, qi, ki, wi, cache_k, cache_v, cache_idx_k, page_table, rel_bias):
    bsz, T = q.shape[:2]
    past = page_table.shape[1] * PAGE_SIZE
    ktop = min(TOPK_MAX, (past + T) // 4)
    ki_past = cache_idx_k[page_table].reshape(bsz, past, IDX_DIM)
    kidx = jnp.concatenate([ki_past, ki], axis=1)

    def gather(sel):
        is_new = (sel >= past)[..., None, None]
        ps = jnp.minimum(sel, past - 1)
        phys = _take_rows(page_table, ps // PAGE_SIZE)
        off = ps % PAGE_SIZE
        ns = jnp.clip(sel - past, 0, T - 1)
        kk = jnp.where(is_new, _take_rows(k, ns), cache_k[phys, off])
        vv = jnp.where(is_new, _take_rows(v, ns), cache_v[phys, off])
        return kk, vv

    qpos = past + jnp.arange(T, dtype=jnp.int32)
    return _sparse_attn(q, qi, wi, qpos, kidx, gather, ktop, rel_bias)


def _route(logits, router_bias):
    n = logits.shape[0]
    s = jax.nn.sigmoid(logits)
    sb = s + router_bias
    gscore = lax.top_k(sb.reshape(n, N_GROUPS, N_EXPERTS // N_GROUPS), 2)[0].sum(-1)
    _, gidx = lax.top_k(gscore, TOPK_GROUPS)
    gmask = jax.nn.one_hot(gidx, N_GROUPS, dtype=F32).sum(1) > 0
    emask = jnp.repeat(gmask, N_EXPERTS // N_GROUPS, axis=1)
    _, eidx = lax.top_k(jnp.where(emask, sb, -jnp.inf), TOP_K)
    gw = jnp.take_along_axis(s, eidx, axis=1)
    gw = gw / jnp.sum(gw, axis=-1, keepdims=True) * ROUTED_SCALE
    a = n * TOP_K
    hit = jnp.sum((eidx[:, :, None] == jnp.arange(N_EXPERTS, dtype=eidx.dtype)).astype(jnp.int32), axis=1)
    incl = jnp.cumsum(hit, axis=0)
    counts = incl[-1]
    pcounts = (counts + MOE_MB - 1) // MOE_MB * MOE_MB
    pend = jnp.cumsum(pcounts)
    pstart = pend - pcounts
    dest = jnp.take_along_axis(incl - hit + pstart[None, :], eidx, axis=1).astype(jnp.int32).reshape(a)
    slot_tok = jnp.full((MOE_NB * MOE_MB,), n, jnp.int32).at[dest].set(
        jnp.arange(a, dtype=jnp.int32) // TOP_K)
    blk_e = jnp.minimum(
        jnp.searchsorted(pend, jnp.arange(MOE_NB, dtype=jnp.int32) * MOE_MB, side='right'),
        N_EXPERTS - 1).astype(jnp.int32)
    return slot_tok, dest.reshape(n, TOP_K), gw, blk_e


def _dispatch(eidx, gw, hit):
    n = eidx.shape[0]
    incl = jnp.cumsum(hit, axis=0)
    counts = incl[-1]
    pcounts = (counts + MOE_MB - 1) // MOE_MB * MOE_MB
    pend = jnp.cumsum(pcounts)
    pstart = pend - pcounts
    dest = jnp.take_along_axis(incl - hit + pstart[None, :], eidx, axis=1).astype(jnp.int32)
    flat = dest.reshape(-1)
    slot_tok = jnp.full((MOE_NB * MOE_MB,), n, jnp.int32).at[flat].set(
        jnp.arange(n * TOP_K, dtype=jnp.int32) // TOP_K)
    slot_w = jnp.zeros((MOE_NB * MOE_MB,), F32).at[flat].set(gw.reshape(-1)).reshape(-1, 1)
    block_start = jnp.arange(MOE_NB, dtype=jnp.int32) * MOE_MB
    blk_e = jnp.minimum(jnp.sum((pend[None, :] <= block_start[:, None]).astype(jnp.int32), axis=1),
                        N_EXPERTS - 1)
    n_used = (pend[-1:] // MOE_MB).astype(jnp.int32)
    return slot_tok, slot_w, dest, blk_e, n_used


def kernel(x_prompt, x_sample, cache_k, cache_v, cache_idx_k, state_conv, state_rnn, page_table,
           p_prompt, p_sample, rel_bias, w_in, conv_w, conv_b, w_rg_a, b_rg_a, w_rg_i, b_rg_i,
           lru_lambda, w_rnn_out, w_att_out, w_out, ln1_g, ln1_b, w_router, router_bias,
           w_e1, w_e3, w_e2, w_s1, w_s3, w_s2, ln2_g, ln2_b, w_ple_gate, w_ple_proj):
    ns = DEC_BATCH * DEC_SEQ
    x_all = jnp.concatenate([x_prompt[0], x_sample.reshape(ns, D_MODEL)], axis=0)
    xb = x_all.astype(BF16)
    win = w_in[0]
    offs = [0]
    for w in SPLITS:
        offs.append(offs[-1] + w)
    xr, yr, q, k, v, qi = [_mm(xb, win, offs[g], SPLITS[g]) for g in range(6)]
    w_small = jnp.pad(win[:, offs[6]:offs[8]], ((0, 0), (0, 128 - IDX_DIM - IDX_HEADS)))
    kw = _mm(xb, w_small, tn=128)
    ki, wi = kw[:, :IDX_DIM], kw[:, IDX_DIM:IDX_DIM + IDX_HEADS]
    w_gate = win[:, offs[8]:]
    g_rnn = _mm(xb, w_gate, 0, D_MODEL)
    g_att = _mm(xb, w_gate, D_MODEL, D_MODEL)

    lw = (conv_w[0], conv_b[0], w_rg_a[0], b_rg_a[0], w_rg_i[0], b_rg_i[0], lru_lambda[0])
    smp = lambda t: t[SEQ:].reshape(DEC_BATCH, DEC_SEQ, -1)
    tmaj = lambda t: t.transpose(1, 0, 2)
    rnn_p, buf_p, h_p = _rglru_prompt(xr, yr, SEQ, *lw)
    rnn_s, buf_s, h_s = _rglru_sample(tmaj(smp(xr)), tmaj(smp(yr)), tmaj(state_conv[0]), state_rnn[0], *lw)
    buf_s = tmaj(buf_s)

    k_p, v_p, ki_p = k[:SEQ], v[:SEQ], ki[:SEQ]
    k_s, v_s, ki_s = smp(k), smp(v), smp(ki)
    att_p = _masked_attn(q[:SEQ], k_p, v_p, _index_select(qi[:SEQ], ki_p, wi[:SEQ]), rel_bias)
    att_s = _sample_attn_pallas(smp(q), k_s, v_s, smp(qi), ki_s, smp(wi),
                                cache_k, cache_v, cache_idx_k, page_table, rel_bias)
    hd = lambda t: t.reshape(t.shape[:-1] + (N_KV, HEAD_DIM))
    k_p, v_p, k_s, v_s = hd(k_p)[None], hd(v_p)[None], hd(k_s), hd(v_s)
    ki_p, buf_p = ki_p[None], buf_p[None]

    rnn_o = jnp.concatenate([rnn_p, tmaj(rnn_s).reshape(ns, D_RNN)], axis=0)
    att_o = jnp.concatenate([att_p, att_s.reshape(ns, D_MODEL).astype(BF16)], axis=0)
    merged = (jax.nn.sigmoid(g_rnn) * _mm(rnn_o, w_rnn_out[0])
              + jax.nn.sigmoid(g_att) * _mm(att_o, w_att_out[0]))
    x1 = _layernorm(DN_ALPHA * x_all + _mm(merged.astype(BF16), w_out[0]), ln1_g[0], ln1_b[0])

    eidx, gw, hit = _router(x1, w_router[0], router_bias[0])
    slot_tok, slot_w, dest, blk_e, n_used = _dispatch(eidx, gw, hit)
    x1b = x1.astype(BF16)
    xpad = jnp.concatenate([x1b, jnp.zeros((1, D_MODEL), BF16)], axis=0)
    yb = _experts(xpad[slot_tok], slot_w, blk_e, n_used, w_e1[0], w_e3[0], w_e2[0])
    routed = jnp.sum(yb[dest].astype(F32), axis=1)
    hs = _mm(x1b, w_s1[0])
    hs = (hs * jax.nn.sigmoid(hs) * _mm(x1b, w_s3[0])).astype(BF16)
    moe = routed + _mm(hs, w_s2[0])
    x2 = _layernorm(DN_ALPHA * x1 + moe, ln2_g[0], ln2_b[0])

    p_all = jnp.concatenate([p_prompt[0, 0], p_sample[0].reshape(ns, -1)], axis=0).astype(BF16)
    y = x2 + jax.nn.sigmoid(_mm(x2.astype(BF16), w_ple_gate[0])) * _mm(p_all, w_ple_proj[0])

    return (y[:SEQ][None], y[SEQ:].reshape(DEC_BATCH, DEC_SEQ, D_MODEL),
            k_p[None], v_p[None], ki_p[None], buf_p[None], h_p[None],
            k_s[None], v_s[None], ki_s[None], buf_s[None], h_s[None])
```

```python
import functools
import math

import jax
import jax.numpy as jnp
from jax import lax
from jax.experimental import pallas as pl
from jax.experimental.pallas import tpu as pltpu

D_MODEL = 4096
SEQ = 8192
DEC_BATCH = 32
DEC_SEQ = 4
PAGE_SIZE = 128
D_RNN = D_MODEL
RNN_BLOCKS = 16
RNN_BW = D_RNN // RNN_BLOCKS
CONV_W = 4
LRU_C = 8.0
N_HEADS = 32
HEAD_DIM = D_MODEL // N_HEADS
N_KV = 8
IDX_HEADS = 16
IDX_DIM = 64
TOPK_MAX = 256
N_BUCKETS = 32
MAX_DIST = 128
N_EXPERTS = 64
TOP_K = 8
N_GROUPS = 8
TOPK_GROUPS = 4
D_EXPERT = 1024
ROUTED_SCALE = 2.5
DN_ALPHA = 2.0 ** 0.25
LN_EPS = 1e-5
SPLITS = (D_RNN, D_RNN, N_HEADS * HEAD_DIM, N_KV * HEAD_DIM, N_KV * HEAD_DIM,
          IDX_HEADS * IDX_DIM, IDX_DIM, IDX_HEADS, D_MODEL, D_MODEL)

F32 = jnp.float32
BF16 = jnp.bfloat16

N_TOK = SEQ + DEC_BATCH * DEC_SEQ
MM_TM = 640
MM_TN = 512
MOE_MB = 512
MOE_NB = -(-(N_TOK * TOP_K) // MOE_MB) + N_EXPERTS
MOE_CE = 256
VMEM_LIMIT = 48 * 1024 * 1024


def _mm_kernel(x_ref, w_ref, o_ref, wbf_ref):
    @pl.when(pl.program_id(1) == 0)
    def _():
        wbf_ref[...] = w_ref[...].astype(BF16)

    o_ref[...] = jnp.dot(x_ref[...], wbf_ref[...], preferred_element_type=F32).astype(o_ref.dtype)


def _mm(x, w, col0=0, ncols=None, out_dtype=F32, tm=MM_TM, tn=MM_TN):
    m, k = x.shape
    ncols = w.shape[1] - col0 if ncols is None else ncols
    tn = min(tn, ncols)
    assert m % tm == 0 and ncols % tn == 0 and col0 % tn == 0
    cb = col0 // tn
    return pl.pallas_call(
        _mm_kernel,
        grid=(ncols // tn, m // tm),
        in_specs=[pl.BlockSpec((tm, k), lambda j, i: (i, 0)),
                  pl.BlockSpec((k, tn), lambda j, i: (0, j + cb))],
        out_specs=pl.BlockSpec((tm, tn), lambda j, i: (i, j)),
        out_shape=jax.ShapeDtypeStruct((m, ncols), out_dtype),
        scratch_shapes=[pltpu.VMEM((k, tn), BF16)],
        compiler_params=pltpu.CompilerParams(dimension_semantics=("arbitrary", "arbitrary"),
                                             vmem_limit_bytes=VMEM_LIMIT),
    )(x, w)


def _expert_changed(be_ref, b):
    return (b == 0) | (be_ref[b] != be_ref[jnp.maximum(b - 1, 0)])


def _expert_up_kernel(be_ref, nu_ref, x_ref, w1_ref, w3_ref, h_ref, w1bf_ref, w3bf_ref):
    b = pl.program_id(1)

    @pl.when(b < nu_ref[0])
    def _():
        @pl.when(_expert_changed(be_ref, b))
        def _():
            w1bf_ref[...] = w1_ref[0].astype(BF16)
            w3bf_ref[...] = w3_ref[0].astype(BF16)

        x = x_ref[...]
        a = jnp.dot(x, w1bf_ref[...], preferred_element_type=F32)
        g = jnp.dot(x, w3bf_ref[...], preferred_element_type=F32)
        h_ref[...] = (a * jax.nn.sigmoid(a) * g).astype(h_ref.dtype)


def _expert_down_kernel(be_ref, nu_ref, h_ref, w2_ref, y_ref, w2bf_ref):
    b = pl.program_id(1)

    @pl.when(b < nu_ref[0])
    def _():
        @pl.when(_expert_changed(be_ref, b))
        def _():
            w2bf_ref[...] = w2_ref[0].astype(BF16)

        y_ref[...] = jnp.dot(h_ref[...], w2bf_ref[...], preferred_element_type=F32)


def _experts(xs, blk_e, n_used, w_e1, w_e3, w_e2):
    d = xs.shape[1]
    nce = D_EXPERT // MOE_CE
    row = lambda c, b, be, nu: jnp.minimum(b, nu[0] - 1)
    h = pl.pallas_call(
        _expert_up_kernel,
        grid_spec=pltpu.PrefetchScalarGridSpec(
            num_scalar_prefetch=2,
            grid=(nce, MOE_NB),
            in_specs=[pl.BlockSpec((MOE_MB, d), lambda *a: (row(*a), 0)),
                      pl.BlockSpec((1, d, MOE_CE), lambda *a: (a[2][row(*a)], 0, a[0])),
                      pl.BlockSpec((1, d, MOE_CE), lambda *a: (a[2][row(*a)], 0, a[0]))],
            out_specs=pl.BlockSpec((MOE_MB, MOE_CE), lambda *a: (row(*a), a[0])),
            scratch_shapes=[pltpu.VMEM((d, MOE_CE), BF16), pltpu.VMEM((d, MOE_CE), BF16)]),
        out_shape=jax.ShapeDtypeStruct((MOE_NB * MOE_MB, D_EXPERT), BF16),
        compiler_params=pltpu.CompilerParams(dimension_semantics=("arbitrary", "arbitrary"),
                                             vmem_limit_bytes=VMEM_LIMIT),
    )(blk_e, n_used, xs, w_e1, w_e3)
    tn = 1024
    return pl.pallas_call(
        _expert_down_kernel,
        grid_spec=pltpu.PrefetchScalarGridSpec(
            num_scalar_prefetch=2,
            grid=(d // tn, MOE_NB),
            in_specs=[pl.BlockSpec((MOE_MB, D_EXPERT), lambda *a: (row(*a), 0)),
                      pl.BlockSpec((1, D_EXPERT, tn), lambda *a: (a[2][row(*a)], 0, a[0]))],
            out_specs=pl.BlockSpec((MOE_MB, tn), lambda *a: (row(*a), a[0])),
            scratch_shapes=[pltpu.VMEM((D_EXPERT, tn), BF16)]),
        out_shape=jax.ShapeDtypeStruct((MOE_NB * MOE_MB, d), F32),
        compiler_params=pltpu.CompilerParams(dimension_semantics=("arbitrary", "arbitrary"),
                                             vmem_limit_bytes=VMEM_LIMIT),
    )(blk_e, n_used, h, w_e2)


LANES = 128


def _router_kernel(x_ref, w_ref, b_ref, eidx_ref, gw_ref, hit_ref):
    gsz = N_EXPERTS // N_GROUPS
    logits = jnp.dot(x_ref[...], w_ref[...], preferred_element_type=F32, precision=lax.Precision.HIGHEST)
    s = jax.nn.sigmoid(logits)
    sb = s + b_ref[...]
    lane = lax.broadcasted_iota(jnp.int32, sb.shape, 1)

    def group_reduce(v, op):
        step = 1
        while step < gsz:
            partner = jnp.where((lane & step) != 0, pltpu.roll(v, step, axis=1),
                                pltpu.roll(v, LANES - step, axis=1))
            v = op(v, partner)
            step *= 2
        return v

    max1 = group_reduce(sb, jnp.maximum)
    first = group_reduce(jnp.where(sb == max1, lane, LANES), jnp.minimum)
    max2 = group_reduce(jnp.where(lane == first, -jnp.inf, sb), jnp.maximum)
    gscore = max1 + max2
    gi = (lane % N_EXPERTS) // gsz
    rank = jnp.zeros(sb.shape, jnp.int32)
    for k in range(1, N_GROUPS):
        other = pltpu.roll(gscore, k * gsz, axis=1)
        oi = (gi - k) % N_GROUPS
        rank = rank + jnp.where((other > gscore) | ((other == gscore) & (oi < gi)), 1, 0)
    cand = jnp.where((rank < TOPK_GROUPS) & (lane < N_EXPERTS), sb, -jnp.inf)
    eidx = jnp.zeros(sb.shape, jnp.int32)
    gw = jnp.zeros(sb.shape, F32)
    hit = jnp.zeros(sb.shape, jnp.int32)
    for r in range(TOP_K):
        best = jnp.max(cand, axis=1, keepdims=True)
        idx = jnp.min(jnp.where(cand == best, lane, LANES), axis=1, keepdims=True)
        chosen = lane == idx
        val = jnp.sum(jnp.where(chosen, s, 0.0), axis=1, keepdims=True)
        eidx = jnp.where(lane == r, idx, eidx)
        gw = jnp.where(lane == r, val, gw)
        hit = jnp.where(chosen, 1, hit)
        cand = jnp.where(chosen, -jnp.inf, cand)
    eidx_ref[...] = eidx
    gw_ref[...] = gw / jnp.sum(gw, axis=1, keepdims=True) * ROUTED_SCALE
    hit_ref[...] = hit


def _router(x, w_router, router_bias, tm=MM_TM):
    n, d = x.shape
    out_spec = pl.BlockSpec((tm, LANES), lambda i: (i, 0))
    eidx, gw, hit = pl.pallas_call(
        _router_kernel,
        grid=(n // tm,),
        in_specs=[pl.BlockSpec((tm, d), lambda i: (i, 0)),
                  pl.BlockSpec((d, LANES), lambda i: (0, 0)),
                  pl.BlockSpec((1, LANES), lambda i: (0, 0))],
        out_specs=[out_spec, out_spec, out_spec],
        out_shape=[jax.ShapeDtypeStruct((n, LANES), jnp.int32), jax.ShapeDtypeStruct((n, LANES), F32),
                   jax.ShapeDtypeStruct((n, LANES), jnp.int32)],
        compiler_params=pltpu.CompilerParams(dimension_semantics=("arbitrary",),
                                             vmem_limit_bytes=VMEM_LIMIT),
    )(x, jnp.concatenate([w_router, w_router], axis=1),
      jnp.concatenate([router_bias, router_bias]).reshape(1, LANES))
    return eidx[:, :TOP_K], gw[:, :TOP_K], hit[:, :N_EXPERTS]


IDX_QB = 128
ATT_TK = 256
ATT_QB = 256
NEG = -1e30
I32_MIN = -2 ** 31
_NT = (((1,), (1,)), ((), ()))


def _index_select_kernel(qi_ref, wi_ref, ki_ref, mask_ref, skey_ref, wb_ref, qh_ref):
    b = pl.program_id(0)
    n_tiles = mask_ref.shape[1]
    nt = (b * IDX_QB + IDX_QB + ATT_TK - 1) // ATT_TK
    qi = (qi_ref[...] * IDX_DIM ** -0.5).astype(BF16)
    w = wi_ref[...] * IDX_HEADS ** -0.5
    for h in range(IDX_HEADS):
        qh_ref[h] = qi[:, h * IDX_DIM:(h + 1) * IDX_DIM]
        wb_ref[h] = jnp.broadcast_to(w[:, h:h + 1], (IDX_QB, ATT_TK))
    row = b * IDX_QB + lax.broadcasted_iota(jnp.int32, (IDX_QB, ATT_TK), 0)
    col = lax.broadcasted_iota(jnp.int32, (IDX_QB, ATT_TK), 1)

    def score_tile(j, carry):
        kt = ki_ref[pl.ds(pl.multiple_of(j * ATT_TK, ATT_TK), ATT_TK), :]
        acc = jnp.zeros((IDX_QB, ATT_TK), F32)
        for h in range(IDX_HEADS):
            s = lax.dot_general(qh_ref[h], kt, _NT, preferred_element_type=F32)
            acc = acc + wb_ref[h] * jnp.maximum(s, 0.0)
        bits = pltpu.bitcast(acc, jnp.int32)
        key = bits ^ ((bits >> 31) & 0x7FFFFFFF)
        skey_ref[j] = jnp.where(col + j * ATT_TK <= row, key, I32_MIN)
        return carry

    lax.fori_loop(0, nt, score_tile, 0)

    half = ATT_TK // 2

    def bit_body(i, ans):
        cand_u = ans | lax.shift_left(jnp.int32(1), 31 - i)
        cand = jnp.broadcast_to(cand_u ^ I32_MIN, (IDX_QB, half))

        def count(j, c):
            t = skey_ref[j]
            return (c + jnp.where(t[:, :half] >= cand, 1, 0) + jnp.where(t[:, half:] >= cand, 1, 0))

        c = lax.fori_loop(0, nt, count, jnp.zeros((IDX_QB, half), jnp.int32))
        cnt = jnp.sum(c, axis=1, keepdims=True)
        return jnp.where(cnt >= TOPK_MAX, cand_u, ans)

    ans = lax.fori_loop(0, 32, bit_body, jnp.zeros((IDX_QB, 1), jnp.int32))
    thr = jnp.broadcast_to(jnp.maximum(ans ^ I32_MIN, I32_MIN + 1), (IDX_QB, ATT_TK))

    def write_tile(j, carry):
        mask_ref[0, j] = jnp.where(skey_ref[j] >= thr, 0.0, NEG).astype(BF16)
        return carry

    def write_empty(j, carry):
        mask_ref[0, j] = jnp.full((IDX_QB, ATT_TK), NEG, BF16)
        return carry

    lax.fori_loop(0, nt, write_tile, 0)
    lax.fori_loop(nt, n_tiles, write_empty, 0)


def _index_select(qi, ki, wi):
    s = qi.shape[0]
    n_tiles = s // ATT_TK
    return pl.pallas_call(
        _index_select_kernel,
        grid=(s // IDX_QB,),
        in_specs=[pl.BlockSpec((IDX_QB, IDX_HEADS * IDX_DIM), lambda b: (b, 0)),
                  pl.BlockSpec((IDX_QB, IDX_HEADS), lambda b: (b, 0)),
                  pl.BlockSpec((s, IDX_DIM), lambda b: (0, 0))],
        out_specs=pl.BlockSpec((1, n_tiles, IDX_QB, ATT_TK), lambda b: (b, 0, 0, 0)),
        out_shape=jax.ShapeDtypeStruct((s // IDX_QB, n_tiles, IDX_QB, ATT_TK), BF16),
        scratch_shapes=[pltpu.VMEM((n_tiles, IDX_QB, ATT_TK), jnp.int32),
                        pltpu.VMEM((IDX_HEADS, IDX_QB, ATT_TK), F32),
                        pltpu.VMEM((IDX_HEADS, IDX_QB, IDX_DIM), BF16)],
        compiler_params=pltpu.CompilerParams(dimension_semantics=("arbitrary",),
                                             vmem_limit_bytes=VMEM_LIMIT),
    )(qi, wi, ki.astype(BF16))


def _masked_attn_kernel(q_ref, k_ref, v_ref, mask_ref, bd_ref, bo_ref, o_ref,
                        qs_ref, m_ref, l_ref, acc_ref):
    grp = N_HEADS // N_KV
    qb = pl.program_id(1)
    q = (q_ref[...] * HEAD_DIM ** -0.5).astype(BF16)
    for hh in range(grp):
        qs_ref[hh] = q[:, hh * HEAD_DIM:(hh + 1) * HEAD_DIM]
    m_ref[...] = jnp.full(m_ref.shape, NEG, F32)
    l_ref[...] = jnp.zeros(l_ref.shape, F32)
    acc_ref[...] = jnp.zeros(acc_ref.shape, F32)

    def tile(j, bias_ref):
        off = pl.multiple_of(j * ATT_TK, ATT_TK)
        kt = k_ref[pl.ds(off, ATT_TK), :]
        vt = v_ref[pl.ds(off, ATT_TK), :]
        for r in range(ATT_QB // IDX_QB):
            rows = slice(r * IDX_QB, (r + 1) * IDX_QB)
            mk = mask_ref[r, j].astype(F32)
            for hh in range(grp):
                s = lax.dot_general(qs_ref[hh, rows], kt, _NT, preferred_element_type=F32) + mk
                if bias_ref is not None:
                    s = s + bias_ref[hh, rows]
                m_old = m_ref[hh, rows]
                m_new = jnp.maximum(m_old, jnp.max(s, axis=1, keepdims=True))
                alpha = jnp.exp(m_old - m_new)
                p = jnp.exp(s - jnp.concatenate([m_new] * (ATT_TK // HEAD_DIM), axis=1))
                l_ref[hh, rows] = alpha * l_ref[hh, rows] + jnp.sum(p, axis=1, keepdims=True)
                acc_ref[hh, rows] = alpha * acc_ref[hh, rows] + jnp.dot(
                    p.astype(BF16), vt, preferred_element_type=F32)
                m_ref[hh, rows] = m_new

    def far_tile(j, carry):
        tile(j, None)
        return carry

    lax.fori_loop(0, jnp.maximum(qb - 1, 0), far_tile, 0)

    @pl.when(qb >= 1)
    def _():
        tile(qb - 1, bo_ref)

    tile(qb, bd_ref)
    for hh in range(grp):
        o_ref[:, hh * HEAD_DIM:(hh + 1) * HEAD_DIM] = (acc_ref[hh] / l_ref[hh]).astype(o_ref.dtype)


def _rel_bias_tiles(rel_bias):
    a = jnp.arange(ATT_QB, dtype=jnp.int32)[:, None]
    c = jnp.arange(ATT_TK, dtype=jnp.int32)[None, :]
    diag = _rel_bias_of(a - c, rel_bias)
    prev = _rel_bias_of(a - c + ATT_TK, rel_bias)
    return diag.transpose(2, 0, 1), prev.transpose(2, 0, 1)


def _masked_attn(q, k, v, mask, rel_bias):
    s = q.shape[0]
    grp = N_HEADS // N_KV
    n_tiles = s // ATT_TK
    bd, bo = _rel_bias_tiles(rel_bias)
    wq = grp * HEAD_DIM
    return pl.pallas_call(
        _masked_attn_kernel,
        grid=(N_KV, s // ATT_QB),
        in_specs=[pl.BlockSpec((ATT_QB, wq), lambda g, i: (i, g)),
                  pl.BlockSpec((s, HEAD_DIM), lambda g, i: (0, g)),
                  pl.BlockSpec((s, HEAD_DIM), lambda g, i: (0, g)),
                  pl.BlockSpec((ATT_QB // IDX_QB, n_tiles, IDX_QB, ATT_TK), lambda g, i: (i, 0, 0, 0)),
                  pl.BlockSpec((grp, ATT_QB, ATT_TK), lambda g, i: (g, 0, 0)),
                  pl.BlockSpec((grp, ATT_QB, ATT_TK), lambda g, i: (g, 0, 0))],
        out_specs=pl.BlockSpec((ATT_QB, wq), lambda g, i: (i, g)),
        out_shape=jax.ShapeDtypeStruct((s, N_HEADS * HEAD_DIM), BF16),
        scratch_shapes=[pltpu.VMEM((grp, ATT_QB, HEAD_DIM), BF16),
                        pltpu.VMEM((grp, ATT_QB, HEAD_DIM), F32),
                        pltpu.VMEM((grp, ATT_QB, HEAD_DIM), F32),
                        pltpu.VMEM((grp, ATT_QB, HEAD_DIM), F32)],
        compiler_params=pltpu.CompilerParams(dimension_semantics=("arbitrary", "arbitrary"),
                                             vmem_limit_bytes=VMEM_LIMIT),
    )(q, k.astype(BF16), v.astype(BF16), mask, bd, bo)


SMP_PG = 8
SMP_TP = 8


def _order_key(score):
    bits = pltpu.bitcast(score, jnp.int32)
    return bits ^ ((bits >> 31) & 0x7FFFFFFF)


def _sample_select_kernel(pt_ref, *refs):
    pages = refs[:SMP_PG]
    kn_ref, qi_ref, w_ref, rep_ref, skey_ref, thr_ref = refs[SMP_PG:]
    p = pl.program_id(1)
    n_tiles = skey_ref.shape[1]
    wide = PAGE_SIZE * N_KV
    qi = qi_ref[0]
    w = jnp.concatenate([w_ref[0].reshape(IDX_HEADS, SMP_TP, PAGE_SIZE)] * N_KV, axis=2)

    def score(kt):
        kx = jnp.dot(rep_ref[...], kt.astype(BF16), preferred_element_type=F32).astype(BF16)
        s = lax.dot_general(qi, kx, _NT, preferred_element_type=F32)
        s = jnp.maximum(s, 0.0).reshape(IDX_HEADS, SMP_TP, wide)
        return _order_key(jnp.sum(w * s, axis=0))

    for i in range(SMP_PG):
        skey_ref[0, p * SMP_PG + i] = score(pages[i][0, 0])

    @pl.when(p == pl.num_programs(1) - 1)
    def _():
        t = lax.broadcasted_iota(jnp.int32, (SMP_TP, wide), 0)
        j = lax.broadcasted_iota(jnp.int32, (SMP_TP, wide), 1) // N_KV
        skey_ref[0, n_tiles - 1] = jnp.where((j <= t) & (j < DEC_SEQ), score(kn_ref[0]), I32_MIN)

        def bit_body(i, ans):
            cand_u = ans | lax.shift_left(jnp.int32(1), 31 - i)
            cand = jnp.broadcast_to(cand_u ^ I32_MIN, (SMP_TP, wide))

            def count(jt, c):
                return c + jnp.where(skey_ref[0, jt] >= cand, 1, 0)

            c = lax.fori_loop(0, n_tiles, count, jnp.zeros((SMP_TP, wide), jnp.int32))
            return jnp.where(jnp.sum(c, axis=1, keepdims=True) >= TOPK_MAX * N_KV, cand_u, ans)

        ans = lax.fori_loop(0, 32, bit_body, jnp.zeros((SMP_TP, 1), jnp.int32))
        thr_ref[0] = jnp.broadcast_to(jnp.maximum(ans ^ I32_MIN, I32_MIN + 1), (SMP_TP, PAGE_SIZE))


def _sample_attn_kernel(pt_ref, *refs):
    kp, vp = refs[:SMP_PG], refs[SMP_PG:2 * SMP_PG]
    kn_ref, vn_ref, q_ref, skey_ref, thr_ref, hm_ref, hl_ref, hn_ref, o_ref, m_ref, l_ref, acc_ref = (
        refs[2 * SMP_PG:])
    p = pl.program_id(1)
    last = p == pl.num_programs(1) - 1
    n_tiles = skey_ref.shape[1]
    wide = PAGE_SIZE * N_KV

    @pl.when(p == 0)
    def _():
        m_ref[...] = jnp.full(m_ref.shape, NEG, F32)
        l_ref[...] = jnp.zeros(l_ref.shape, F32)
        acc_ref[...] = jnp.zeros(acc_ref.shape, F32)

    q = q_ref[0]
    thr = jnp.concatenate([thr_ref[0]] * N_KV, axis=1)

    def update(tile, k_rows, v_rows, head_mask):
        sel = jnp.where(skey_ref[0, tile] >= thr, 0.0, NEG)
        s = lax.dot_general(q, k_rows.astype(BF16), _NT, preferred_element_type=F32)
        s = s + jnp.concatenate([sel] * N_HEADS, axis=0) + head_mask
        m_old = m_ref[...]
        m_new = jnp.maximum(m_old, jnp.max(s, axis=1, keepdims=True))
        alpha = jnp.exp(m_old - m_new)
        pr = jnp.exp(s - jnp.concatenate([m_new] * (wide // HEAD_DIM), axis=1))
        l_ref[...] = alpha * l_ref[...] + jnp.sum(pr, axis=1, keepdims=True)
        acc_ref[...] = alpha * acc_ref[...] + jnp.dot(pr.astype(BF16), v_rows.astype(BF16),
                                                      preferred_element_type=F32)
        m_ref[...] = m_new

    for i in range(SMP_PG):
        head_mask = jnp.where(last, hl_ref[...], hm_ref[...]) if i == SMP_PG - 1 else hm_ref[...]
        update(p * SMP_PG + i, kp[i][0, 0].reshape(wide, HEAD_DIM), vp[i][0, 0].reshape(wide, HEAD_DIM),
               head_mask)

    @pl.when(last)
    def _():
        update(n_tiles - 1, kn_ref[0], vn_ref[0], hn_ref[...])
        o_ref[0] = acc_ref[...] / l_ref[...]


def _sample_attn_pallas(q, k, v, qi, ki, wi, cache_k, cache_v, cache_idx_k, page_table, rel_bias):
    bsz, nq = q.shape[:2]
    n_pages = page_table.shape[1]
    past = n_pages * PAGE_SIZE
    grp = N_HEADS // N_KV
    n_steps = n_pages // SMP_PG
    tpad = ((0, 0), (0, 0), (0, SMP_TP - nq), (0, 0))
    kpad = ((0, 0), (0, PAGE_SIZE - nq), (0, 0))
    qi_r = jnp.pad((qi * IDX_DIM ** -0.5).reshape(bsz, nq, IDX_HEADS, IDX_DIM).transpose(0, 2, 1, 3), tpad)
    qi_r = qi_r.reshape(bsz, IDX_HEADS * SMP_TP, IDX_DIM).astype(BF16)
    w_r = jnp.pad((wi * IDX_HEADS ** -0.5).transpose(0, 2, 1), ((0, 0), (0, 0), (0, SMP_TP - nq)))
    w_r = jnp.broadcast_to(w_r.reshape(bsz, IDX_HEADS * SMP_TP, 1), (bsz, IDX_HEADS * SMP_TP, PAGE_SIZE))
    page_specs = lambda *tail: [
        pl.BlockSpec((1, 1, PAGE_SIZE) + tail, functools.partial(
            lambda b, p, pt, i: (0, pt[b, p * SMP_PG + i], 0) + (0,) * len(tail), i=i))
        for i in range(SMP_PG)]
    per_batch = lambda *shape: pl.BlockSpec((1,) + shape, lambda b, p, pt: (b,) + (0,) * len(shape))
    const = lambda *shape: pl.BlockSpec(shape, lambda b, p, pt: (0,) * len(shape))
    wide = PAGE_SIZE * N_KV
    col_pos = jnp.arange(wide, dtype=jnp.int32) // N_KV
    col_kv = jnp.arange(wide, dtype=jnp.int32) % N_KV
    rep = (col_pos[:, None] == jnp.arange(PAGE_SIZE, dtype=jnp.int32)[None, :]).astype(BF16)
    skey, thr = pl.pallas_call(
        _sample_select_kernel,
        grid_spec=pltpu.PrefetchScalarGridSpec(
            num_scalar_prefetch=1,
            grid=(bsz, n_steps),
            in_specs=page_specs(IDX_DIM) + [per_batch(PAGE_SIZE, IDX_DIM),
                                            per_batch(IDX_HEADS * SMP_TP, IDX_DIM),
                                            per_batch(IDX_HEADS * SMP_TP, PAGE_SIZE),
                                            const(wide, PAGE_SIZE)],
            out_specs=[per_batch(n_pages + 1, SMP_TP, wide), per_batch(SMP_TP, PAGE_SIZE)]),
        out_shape=[jax.ShapeDtypeStruct((bsz, n_pages + 1, SMP_TP, wide), jnp.int32),
                   jax.ShapeDtypeStruct((bsz, SMP_TP, PAGE_SIZE), jnp.int32)],
        compiler_params=pltpu.CompilerParams(dimension_semantics=("arbitrary", "arbitrary"),
                                             vmem_limit_bytes=VMEM_LIMIT),
    )(page_table, *([cache_idx_k] * SMP_PG), jnp.pad(ki, kpad), qi_r, w_r, rep)

    n_rows = N_HEADS * SMP_TP
    q_r = jnp.pad((q * HEAD_DIM ** -0.5).reshape(bsz, nq, N_HEADS, HEAD_DIM).transpose(0, 2, 1, 3), tpad)
    q_r = q_r.reshape(bsz, n_rows, HEAD_DIM).astype(BF16)
    row_kv = jnp.arange(n_rows, dtype=jnp.int32) // (grp * SMP_TP)
    head_mask = jnp.where(row_kv[:, None] == col_kv[None, :], 0.0, NEG).astype(F32)
    t = jnp.arange(SMP_TP, dtype=jnp.int32)[:, None]
    pos = jnp.arange(PAGE_SIZE, dtype=jnp.int32)[None, :]
    bias_of = lambda dist: jnp.repeat(
        _rel_bias_of(dist, rel_bias).transpose(2, 0, 1).reshape(n_rows, PAGE_SIZE), N_KV, axis=1)
    mask_last = head_mask + bias_of(PAGE_SIZE + t - pos)
    mask_new = head_mask + bias_of(t - pos)
    new_rows = lambda a: jnp.pad(a.reshape(bsz, nq * N_KV, HEAD_DIM), ((0, 0), (0, wide - nq * N_KV), (0, 0)))
    o = pl.pallas_call(
        _sample_attn_kernel,
        grid_spec=pltpu.PrefetchScalarGridSpec(
            num_scalar_prefetch=1,
            grid=(bsz, n_steps),
            in_specs=page_specs(N_KV, HEAD_DIM) + page_specs(N_KV, HEAD_DIM) + [
                per_batch(wide, HEAD_DIM), per_batch(wide, HEAD_DIM),
                per_batch(n_rows, HEAD_DIM),
                per_batch(n_pages + 1, SMP_TP, wide), per_batch(SMP_TP, PAGE_SIZE),
                const(n_rows, wide), const(n_rows, wide), const(n_rows, wide)],
            out_specs=per_batch(n_rows, HEAD_DIM),
            scratch_shapes=[pltpu.VMEM((n_rows, HEAD_DIM), F32)] * 3),
        out_shape=jax.ShapeDtypeStruct((bsz, n_rows, HEAD_DIM), F32),
        compiler_params=pltpu.CompilerParams(dimension_semantics=("arbitrary", "arbitrary"),
                                             vmem_limit_bytes=VMEM_LIMIT),
    )(page_table, *([cache_k] * SMP_PG), *([cache_v] * SMP_PG),
      new_rows(k), new_rows(v), q_r, skey, thr, head_mask, mask_last, mask_new)
    o = o.reshape(bsz, N_HEADS, SMP_TP, HEAD_DIM)[:, :, :nq]
    return o.transpose(0, 2, 1, 3).reshape(bsz, nq, N_HEADS * HEAD_DIM)


LRU_TT = 512
_GELU_C = math.sqrt(2.0 / math.pi)


def _gelu_tanh(y):
    return 0.5 * y * (1.0 + jnp.tanh(_GELU_C * (y + 0.044715 * (y * y * y))))


def _lru_coeffs(xc, wa_ref, ba_ref, wi_ref, bi_ref, lam_ref):
    xb = xc.astype(BF16)
    r = jax.nn.sigmoid(jnp.dot(xb, wa_ref[0].astype(BF16), preferred_element_type=F32) + ba_ref[0])
    i = jax.nn.sigmoid(jnp.dot(xb, wi_ref[0].astype(BF16), preferred_element_type=F32) + bi_ref[0])
    z = -lam_ref[...]
    softplus = jnp.maximum(z, 0.0) + jnp.log1p(jnp.exp(-jnp.abs(z)))
    log_a = -LRU_C * r * softplus
    t = jnp.tanh(log_a)
    u = jnp.sqrt(-2.0 * t / (1.0 - t)) * (i * xc)
    return jnp.exp(log_a), u


def _rglru_prompt_kernel(x_ref, y_ref, cw_ref, cb_ref, wa_ref, ba_ref, wi_ref, bi_ref, lam_ref,
                         o_ref, nb_ref, hl_ref, xbuf_ref, hc_ref):
    c = pl.program_id(1)
    tt = x_ref.shape[0]

    @pl.when(c == 0)
    def _():
        xbuf_ref[0:8] = jnp.zeros((8, RNN_BW), F32)
        hc_ref[...] = jnp.zeros(hc_ref.shape, F32)

    x = x_ref[...]
    xbuf_ref[8:8 + tt] = x
    xc = cb_ref[...] + cw_ref[CONV_W - 1:CONV_W] * x
    for j in range(CONV_W - 1):
        xc = xc + cw_ref[j:j + 1] * xbuf_ref[8 - (CONV_W - 1) + j:8 - (CONV_W - 1) + j + tt]
    a, u = _lru_coeffs(xc, wa_ref, ba_ref, wi_ref, bi_ref, lam_ref)
    row = lax.broadcasted_iota(jnp.int32, (tt, RNN_BW), 0)
    s = 1
    while s < tt:
        keep = row >= s
        a_prev = jnp.where(keep, pltpu.roll(a, s, axis=0), 1.0)
        u_prev = jnp.where(keep, pltpu.roll(u, s, axis=0), 0.0)
        u = a * u_prev + u
        a = a * a_prev
        s *= 2
    h = a * hc_ref[0:1] + u
    o_ref[...] = (h * _gelu_tanh(y_ref[...])).astype(o_ref.dtype)
    hc_ref[0:1] = h[tt - 1:tt]
    xbuf_ref[0:8] = x[tt - 8:tt]

    @pl.when(c == pl.num_programs(1) - 1)
    def _():
        hl_ref[...] = h[tt - 1:tt]
        nb_ref[...] = x[tt - (CONV_W - 1):tt]


def _rglru_prompt(xr, yr, s, conv_w, conv_b, w_rg_a, b_rg_a, w_rg_i, b_rg_i, lru_lambda):
    vec = lambda t: t.reshape(1, D_RNN)
    blk = lambda t: t.reshape(RNN_BLOCKS, 1, RNN_BW)
    row_spec = pl.BlockSpec((LRU_TT, RNN_BW), lambda n, c: (c, n))
    vec_spec = pl.BlockSpec((1, RNN_BW), lambda n, c: (0, n))
    w_spec = pl.BlockSpec((1, RNN_BW, RNN_BW), lambda n, c: (n, 0, 0))
    b_spec = pl.BlockSpec((1, 1, RNN_BW), lambda n, c: (n, 0, 0))
    return pl.pallas_call(
        _rglru_prompt_kernel,
        grid=(RNN_BLOCKS, s // LRU_TT),
        in_specs=[row_spec, row_spec, pl.BlockSpec((CONV_W, RNN_BW), lambda n, c: (0, n)), vec_spec,
                  w_spec, b_spec, w_spec, b_spec, vec_spec],
        out_specs=[row_spec, pl.BlockSpec((CONV_W - 1, RNN_BW), lambda n, c: (0, n)), vec_spec],
        out_shape=[jax.ShapeDtypeStruct((s, D_RNN), BF16),
                   jax.ShapeDtypeStruct((CONV_W - 1, D_RNN), F32),
                   jax.ShapeDtypeStruct((1, D_RNN), F32)],
        scratch_shapes=[pltpu.VMEM((8 + LRU_TT, RNN_BW), F32), pltpu.VMEM((8, RNN_BW), F32)],
        compiler_params=pltpu.CompilerParams(dimension_semantics=("arbitrary", "arbitrary"),
                                             vmem_limit_bytes=VMEM_LIMIT),
    )(xr, yr, conv_w, vec(conv_b), w_rg_a, blk(b_rg_a), w_rg_i, blk(b_rg_i), vec(lru_lambda))


def _rglru_sample_kernel(x_ref, y_ref, buf_ref, h0_ref, cw_ref, cb_ref, wa_ref, ba_ref, wi_ref, bi_ref,
                         lam_ref, o_ref, nb_ref, hl_ref):
    nt, nb = x_ref.shape[0], x_ref.shape[1]
    xs = [buf_ref[j] for j in range(CONV_W - 1)] + [x_ref[t] for t in range(nt)]
    xc = []
    for t in range(nt):
        acc = cb_ref[...] + cw_ref[0:1] * xs[t]
        for j in range(1, CONV_W):
            acc = acc + cw_ref[j:j + 1] * xs[t + j]
        xc.append(acc)
    a, u = _lru_coeffs(jnp.concatenate(xc, axis=0), wa_ref, ba_ref, wi_ref, bi_ref, lam_ref)
    h = h0_ref[...]
    for t in range(nt):
        h = a[t * nb:(t + 1) * nb] * h + u[t * nb:(t + 1) * nb]
        o_ref[t] = (h * _gelu_tanh(y_ref[t])).astype(o_ref.dtype)
    hl_ref[...] = h
    for j in range(CONV_W - 1):
        nb_ref[j] = xs[nt + j]


def _rglru_sample(xr, yr, buf, h0, conv_w, conv_b, w_rg_a, b_rg_a, w_rg_i, b_rg_i, lru_lambda):
    nt, nb, _ = xr.shape
    vec = lambda t: t.reshape(1, D_RNN)
    blk = lambda t: t.reshape(RNN_BLOCKS, 1, RNN_BW)
    seq_spec = lambda n_rows: pl.BlockSpec((n_rows, nb, RNN_BW), lambda n: (0, 0, n))
    vec_spec = pl.BlockSpec((1, RNN_BW), lambda n: (0, n))
    w_spec = pl.BlockSpec((1, RNN_BW, RNN_BW), lambda n: (n, 0, 0))
    b_spec = pl.BlockSpec((1, 1, RNN_BW), lambda n: (n, 0, 0))
    h_spec = pl.BlockSpec((nb, RNN_BW), lambda n: (0, n))
    return pl.pallas_call(
        _rglru_sample_kernel,
        grid=(RNN_BLOCKS,),
        in_specs=[seq_spec(nt), seq_spec(nt), seq_spec(CONV_W - 1), h_spec,
                  pl.BlockSpec((CONV_W, RNN_BW), lambda n: (0, n)), vec_spec,
                  w_spec, b_spec, w_spec, b_spec, vec_spec],
        out_specs=[seq_spec(nt), seq_spec(CONV_W - 1), h_spec],
        out_shape=[jax.ShapeDtypeStruct((nt, nb, D_RNN), BF16),
                   jax.ShapeDtypeStruct((CONV_W - 1, nb, D_RNN), F32),
                   jax.ShapeDtypeStruct((nb, D_RNN), F32)],
        compiler_params=pltpu.CompilerParams(dimension_semantics=("arbitrary",),
                                             vmem_limit_bytes=VMEM_LIMIT),
    )(xr, yr, buf, h0, conv_w, vec(conv_b), w_rg_a, blk(b_rg_a), w_rg_i, blk(b_rg_i), vec(lru_lambda))


def _layernorm(x, g, b):
    mu = jnp.mean(x, axis=-1, keepdims=True)
    var = jnp.mean(jnp.square(x - mu), axis=-1, keepdims=True)
    return (x - mu) * lax.rsqrt(var + LN_EPS) * g + b


def _rel_bucket(dist):
    max_exact = N_BUCKETS // 2
    d = jnp.maximum(dist, 0)
    large = max_exact + (jnp.log(jnp.maximum(d, 1).astype(F32) / max_exact)
                         / math.log(MAX_DIST / max_exact) * (N_BUCKETS - max_exact)).astype(jnp.int32)
    large = jnp.minimum(large, N_BUCKETS - 1)
    return jnp.where(d < max_exact, d, large)


def _rel_bias_of(dist, rel_bias):
    onehot = jax.nn.one_hot(_rel_bucket(dist), N_BUCKETS, dtype=F32)
    return jnp.dot(onehot, rel_bias - rel_bias[N_BUCKETS - 1], precision=lax.Precision.HIGHEST)


def _dispatch(eidx, hit):
    n = eidx.shape[0]
    incl = jnp.cumsum(hit, axis=0)
    counts = incl[-1]
    pcounts = (counts + MOE_MB - 1) // MOE_MB * MOE_MB
    pend = jnp.cumsum(pcounts)
    pstart = pend - pcounts
    dest = jnp.take_along_axis(incl - hit + pstart[None, :], eidx, axis=1).astype(jnp.int32)
    flat = dest.reshape(-1)
    slot_tok = jnp.full((MOE_NB * MOE_MB,), n, jnp.int32).at[flat].set(
        jnp.arange(n * TOP_K, dtype=jnp.int32) // TOP_K)
    block_start = jnp.arange(MOE_NB, dtype=jnp.int32) * MOE_MB
    blk_e = jnp.minimum(jnp.sum((pend[None, :] <= block_start[:, None]).astype(jnp.int32), axis=1),
                        N_EXPERTS - 1)
    n_used = (pend[-1:] // MOE_MB).astype(jnp.int32)
    return slot_tok, dest, blk_e, n_used


def kernel(x_prompt, x_sample, cache_k, cache_v, cache_idx_k, state_conv, state_rnn, page_table,
           p_prompt, p_sample, rel_bias, w_in, conv_w, conv_b, w_rg_a, b_rg_a, w_rg_i, b_rg_i,
           lru_lambda, w_rnn_out, w_att_out, w_out, ln1_g, ln1_b, w_router, router_bias,
           w_e1, w_e3, w_e2, w_s1, w_s3, w_s2, ln2_g, ln2_b, w_ple_gate, w_ple_proj):
    ns = DEC_BATCH * DEC_SEQ
    x_all = jnp.concatenate([x_prompt[0], x_sample.reshape(ns, D_MODEL)], axis=0)
    xb = x_all.astype(BF16)
    win = w_in[0]
    offs = [0]
    for w in SPLITS:
        offs.append(offs[-1] + w)
    xr, yr, q, k, v, qi = [_mm(xb, win, offs[g], SPLITS[g]) for g in range(6)]
    w_small = jnp.pad(win[:, offs[6]:offs[8]], ((0, 0), (0, 128 - IDX_DIM - IDX_HEADS)))
    kw = _mm(xb, w_small, tn=128)
    ki, wi = kw[:, :IDX_DIM], kw[:, IDX_DIM:IDX_DIM + IDX_HEADS]
    w_gate = win[:, offs[8]:]
    g_rnn = _mm(xb, w_gate, 0, D_MODEL)
    g_att = _mm(xb, w_gate, D_MODEL, D_MODEL)

    lw = (conv_w[0], conv_b[0], w_rg_a[0], b_rg_a[0], w_rg_i[0], b_rg_i[0], lru_lambda[0])
    smp = lambda t: t[SEQ:].reshape(DEC_BATCH, DEC_SEQ, -1)
    tmaj = lambda t: t.transpose(1, 0, 2)
    rnn_p, buf_p, h_p = _rglru_prompt(xr, yr, SEQ, *lw)
    rnn_s, buf_s, h_s = _rglru_sample(tmaj(smp(xr)), tmaj(smp(yr)), tmaj(state_conv[0]), state_rnn[0], *lw)
    buf_s = tmaj(buf_s)

    k_p, v_p, ki_p = k[:SEQ], v[:SEQ], ki[:SEQ]
    k_s, v_s, ki_s = smp(k), smp(v), smp(ki)
    att_p = _masked_attn(q[:SEQ], k_p, v_p, _index_select(qi[:SEQ], ki_p, wi[:SEQ]), rel_bias)
    att_s = _sample_attn_pallas(smp(q), k_s, v_s, smp(qi), ki_s, smp(wi),
                                cache_k, cache_v, cache_idx_k, page_table, rel_bias)
    hd = lambda t: t.reshape(t.shape[:-1] + (N_KV, HEAD_DIM))
    k_p, v_p, k_s, v_s = hd(k_p)[None], hd(v_p)[None], hd(k_s), hd(v_s)
    ki_p, buf_p = ki_p[None], buf_p[None]

    rnn_o = jnp.concatenate([rnn_p, tmaj(rnn_s).reshape(ns, D_RNN)], axis=0)
    att_o = jnp.concatenate([att_p, att_s.reshape(ns, D_MODEL).astype(BF16)], axis=0)
    merged = (jax.nn.sigmoid(g_rnn) * _mm(rnn_o, w_rnn_out[0])
              + jax.nn.sigmoid(g_att) * _mm(att_o, w_att_out[0]))
    x1 = _layernorm(DN_ALPHA * x_all + _mm(merged.astype(BF16), w_out[0]), ln1_g[0], ln1_b[0])

    eidx, gw, hit = _router(x1, w_router[0], router_bias[0])
    slot_tok, dest, blk_e, n_used = _dispatch(eidx, hit)
    x1b = x1.astype(BF16)
    xpad = jnp.concatenate([x1b, jnp.zeros((1, D_MODEL), BF16)], axis=0)
    yb = _experts(xpad[slot_tok], blk_e, n_used, w_e1[0], w_e3[0], w_e2[0])
    routed = jnp.sum(yb[dest] * gw[:, :, None], axis=1)
    hs = _mm(x1b, w_s1[0])
    hs = (hs * jax.nn.sigmoid(hs) * _mm(x1b, w_s3[0])).astype(BF16)
    moe = routed + _mm(hs, w_s2[0])
    x2 = _layernorm(DN_ALPHA * x1 + moe, ln2_g[0], ln2_b[0])

    p_all = jnp.concatenate([p_prompt[0, 0], p_sample[0].reshape(ns, -1)], axis=0).astype(BF16)
    y = x2 + jax.nn.sigmoid(_mm(x2.astype(BF16), w_ple_gate[0])) * _mm(p_all, w_ple_proj[0])

    return (y[:SEQ][None], y[SEQ:].reshape(DEC_BATCH, DEC_SEQ, D_MODEL),
            k_p[None], v_p[None], ki_p[None], buf_p[None], h_p[None],
            k_s[None], v_s[None], ki_s[None], buf_s[None], h_s[None])
```

```python
import functools
import math

import jax
import jax.numpy as jnp
from jax import lax
from jax.experimental import pallas as pl
from jax.experimental.pallas import tpu as pltpu

D_MODEL = 4096
SEQ = 8192
DEC_BATCH = 32
DEC_SEQ = 4
PAGE_SIZE = 128
D_RNN = D_MODEL
RNN_BLOCKS = 16
RNN_BW = D_RNN // RNN_BLOCKS
CONV_W = 4
LRU_C = 8.0
N_HEADS = 32
HEAD_DIM = D_MODEL // N_HEADS
N_KV = 8
IDX_HEADS = 16
IDX_DIM = 64
TOPK_MAX = 256
N_BUCKETS = 32
MAX_DIST = 128
N_EXPERTS = 64
TOP_K = 8
N_GROUPS = 8
TOPK_GROUPS = 4
D_EXPERT = 1024
ROUTED_SCALE = 2.5
DN_ALPHA = 2.0 ** 0.25
LN_EPS = 1e-5
SPLITS = (D_RNN, D_RNN, N_HEADS * HEAD_DIM, N_KV * HEAD_DIM, N_KV * HEAD_DIM,
          IDX_HEADS * IDX_DIM, IDX_DIM, IDX_HEADS, D_MODEL, D_MODEL)

F32 = jnp.float32
BF16 = jnp.bfloat16

N_TOK = SEQ + DEC_BATCH * DEC_SEQ
MM_TM = 1040
MM_TN = 512
MOE_MB = 512
MOE_NB = -(-(N_TOK * TOP_K) // MOE_MB) + N_EXPERTS
MOE_CE = 256
VMEM_LIMIT = 48 * 1024 * 1024


def _mm_kernel(x_ref, w_ref, o_ref, wbf_ref):
    @pl.when(pl.program_id(1) == 0)
    def _():
        wbf_ref[...] = w_ref[...].astype(BF16)

    o_ref[...] = jnp.dot(x_ref[...], wbf_ref[...], preferred_element_type=F32).astype(o_ref.dtype)


def _mm(x, w, col0=0, ncols=None, out_dtype=F32, tm=MM_TM, tn=MM_TN):
    m, k = x.shape
    ncols = w.shape[1] - col0 if ncols is None else ncols
    tn = min(tn, ncols)
    assert m % tm == 0 and ncols % tn == 0 and col0 % tn == 0
    cb = col0 // tn
    return pl.pallas_call(
        _mm_kernel,
        grid=(ncols // tn, m // tm),
        in_specs=[pl.BlockSpec((tm, k), lambda j, i: (i, 0)),
                  pl.BlockSpec((k, tn), lambda j, i: (0, j + cb))],
        out_specs=pl.BlockSpec((tm, tn), lambda j, i: (i, j)),
        out_shape=jax.ShapeDtypeStruct((m, ncols), out_dtype),
        scratch_shapes=[pltpu.VMEM((k, tn), BF16)],
        compiler_params=pltpu.CompilerParams(dimension_semantics=("arbitrary", "arbitrary"),
                                             vmem_limit_bytes=VMEM_LIMIT),
    )(x, w)


def _expert_changed(be_ref, b):
    return (b == 0) | (be_ref[b] != be_ref[jnp.maximum(b - 1, 0)])


def _expert_up_kernel(be_ref, nu_ref, x_ref, w1_ref, w3_ref, h_ref, w1bf_ref, w3bf_ref):
    b = pl.program_id(1)

    @pl.when(b < nu_ref[0])
    def _():
        @pl.when(_expert_changed(be_ref, b))
        def _():
            w1bf_ref[...] = w1_ref[0].astype(BF16)
            w3bf_ref[...] = w3_ref[0].astype(BF16)

        x = x_ref[...]
        a = jnp.dot(x, w1bf_ref[...], preferred_element_type=F32)
        g = jnp.dot(x, w3bf_ref[...], preferred_element_type=F32)
        h_ref[...] = (a * jax.nn.sigmoid(a) * g).astype(h_ref.dtype)


def _expert_down_kernel(be_ref, nu_ref, h_ref, w2_ref, y_ref, w2bf_ref):
    b = pl.program_id(1)

    @pl.when(b < nu_ref[0])
    def _():
        @pl.when(_expert_changed(be_ref, b))
        def _():
            w2bf_ref[...] = w2_ref[0].astype(BF16)

        y_ref[...] = jnp.dot(h_ref[...], w2bf_ref[...], preferred_element_type=F32).astype(y_ref.dtype)


def _experts(xs, blk_e, n_used, w_e1, w_e3, w_e2):
    d = xs.shape[1]
    nce = D_EXPERT // MOE_CE
    row = lambda c, b, be, nu: jnp.minimum(b, nu[0] - 1)
    h = pl.pallas_call(
        _expert_up_kernel,
        grid_spec=pltpu.PrefetchScalarGridSpec(
            num_scalar_prefetch=2,
            grid=(nce, MOE_NB),
            in_specs=[pl.BlockSpec((MOE_MB, d), lambda *a: (row(*a), 0)),
                      pl.BlockSpec((1, d, MOE_CE), lambda *a: (a[2][row(*a)], 0, a[0])),
                      pl.BlockSpec((1, d, MOE_CE), lambda *a: (a[2][row(*a)], 0, a[0]))],
            out_specs=pl.BlockSpec((MOE_MB, MOE_CE), lambda *a: (row(*a), a[0])),
            scratch_shapes=[pltpu.VMEM((d, MOE_CE), BF16), pltpu.VMEM((d, MOE_CE), BF16)]),
        out_shape=jax.ShapeDtypeStruct((MOE_NB * MOE_MB, D_EXPERT), BF16),
        compiler_params=pltpu.CompilerParams(dimension_semantics=("arbitrary", "arbitrary"),
                                             vmem_limit_bytes=VMEM_LIMIT),
    )(blk_e, n_used, xs, w_e1, w_e3)
    tn = 1024
    return pl.pallas_call(
        _expert_down_kernel,
        grid_spec=pltpu.PrefetchScalarGridSpec(
            num_scalar_prefetch=2,
            grid=(d // tn, MOE_NB),
            in_specs=[pl.BlockSpec((MOE_MB, D_EXPERT), lambda *a: (row(*a), 0)),
                      pl.BlockSpec((1, D_EXPERT, tn), lambda *a: (a[2][row(*a)], 0, a[0]))],
            out_specs=pl.BlockSpec((MOE_MB, tn), lambda *a: (row(*a), a[0])),
            scratch_shapes=[pltpu.VMEM((D_EXPERT, tn), BF16)]),
        out_shape=jax.ShapeDtypeStruct((MOE_NB * MOE_MB, d), BF16),
        compiler_params=pltpu.CompilerParams(dimension_semantics=("arbitrary", "arbitrary"),
                                             vmem_limit_bytes=VMEM_LIMIT),
    )(blk_e, n_used, h, w_e2)


LANES = 128


def _router_kernel(x_ref, w_ref, b_ref, eidx_ref, gw_ref, hit_ref):
    gsz = N_EXPERTS // N_GROUPS
    logits = jnp.dot(x_ref[...], w_ref[...], preferred_element_type=F32, precision=lax.Precision.HIGHEST)
    s = jax.nn.sigmoid(logits)
    sb = s + b_ref[...]
    lane = lax.broadcasted_iota(jnp.int32, sb.shape, 1)

    def group_reduce(v, op):
        step = 1
        while step < gsz:
            partner = jnp.where((lane & step) != 0, pltpu.roll(v, step, axis=1),
                                pltpu.roll(v, LANES - step, axis=1))
            v = op(v, partner)
            step *= 2
        return v

    max1 = group_reduce(sb, jnp.maximum)
    first = group_reduce(jnp.where(sb == max1, lane, LANES), jnp.minimum)
    max2 = group_reduce(jnp.where(lane == first, -jnp.inf, sb), jnp.maximum)
    gscore = max1 + max2
    gi = (lane % N_EXPERTS) // gsz
    rank = jnp.zeros(sb.shape, jnp.int32)
    for k in range(1, N_GROUPS):
        other = pltpu.roll(gscore, k * gsz, axis=1)
        oi = (gi - k) % N_GROUPS
        rank = rank + jnp.where((other > gscore) | ((other == gscore) & (oi < gi)), 1, 0)
    cand = jnp.where((rank < TOPK_GROUPS) & (lane < N_EXPERTS), sb, -jnp.inf)
    eidx = jnp.zeros(sb.shape, jnp.int32)
    gw = jnp.zeros(sb.shape, F32)
    hit = jnp.zeros(sb.shape, jnp.int32)
    for r in range(TOP_K):
        best = jnp.max(cand, axis=1, keepdims=True)
        idx = jnp.min(jnp.where(cand == best, lane, LANES), axis=1, keepdims=True)
        chosen = lane == idx
        val = jnp.sum(jnp.where(chosen, s, 0.0), axis=1, keepdims=True)
        eidx = jnp.where(lane == r, idx, eidx)
        gw = jnp.where(lane == r, val, gw)
        hit = jnp.where(chosen, 1, hit)
        cand = jnp.where(chosen, -jnp.inf, cand)
    eidx_ref[...] = eidx
    gw_ref[...] = gw / jnp.sum(gw, axis=1, keepdims=True) * ROUTED_SCALE
    hit_ref[...] = hit


def _router(x, w_router, router_bias, tm=MM_TM):
    n, d = x.shape
    out_spec = pl.BlockSpec((tm, LANES), lambda i: (i, 0))
    eidx, gw, hit = pl.pallas_call(
        _router_kernel,
        grid=(n // tm,),
        in_specs=[pl.BlockSpec((tm, d), lambda i: (i, 0)),
                  pl.BlockSpec((d, LANES), lambda i: (0, 0)),
                  pl.BlockSpec((1, LANES), lambda i: (0, 0))],
        out_specs=[out_spec, out_spec, out_spec],
        out_shape=[jax.ShapeDtypeStruct((n, LANES), jnp.int32), jax.ShapeDtypeStruct((n, LANES), F32),
                   jax.ShapeDtypeStruct((n, LANES), jnp.int32)],
        compiler_params=pltpu.CompilerParams(dimension_semantics=("arbitrary",),
                                             vmem_limit_bytes=VMEM_LIMIT),
    )(x, jnp.concatenate([w_router, w_router], axis=1),
      jnp.concatenate([router_bias, router_bias]).reshape(1, LANES))
    return eidx[:, :TOP_K], gw[:, :TOP_K], hit[:, :N_EXPERTS]


IDX_QB = 128
ATT_TK = 256
ATT_QB = 256
NEG = -1e30
I32_MIN = -2 ** 31
_NT = (((1,), (1,)), ((), ()))


def _index_select_kernel(qi_ref, wi_ref, ki_ref, mask_ref, skey_ref, wb_ref, qh_ref):
    b = pl.program_id(0)
    n_tiles = mask_ref.shape[1]
    nt = (b * IDX_QB + IDX_QB + ATT_TK - 1) // ATT_TK
    qi = (qi_ref[...] * IDX_DIM ** -0.5).astype(BF16)
    w = wi_ref[...] * IDX_HEADS ** -0.5
    for h in range(IDX_HEADS):
        qh_ref[h] = qi[:, h * IDX_DIM:(h + 1) * IDX_DIM]
        wb_ref[h] = jnp.broadcast_to(w[:, h:h + 1], (IDX_QB, ATT_TK))
    row = b * IDX_QB + lax.broadcasted_iota(jnp.int32, (IDX_QB, ATT_TK), 0)
    col = lax.broadcasted_iota(jnp.int32, (IDX_QB, ATT_TK), 1)

    def score_tile(j, carry):
        kt = ki_ref[pl.ds(pl.multiple_of(j * ATT_TK, ATT_TK), ATT_TK), :]
        acc = jnp.zeros((IDX_QB, ATT_TK), F32)
        for h in range(IDX_HEADS):
            s = lax.dot_general(qh_ref[h], kt, _NT, preferred_element_type=F32)
            acc = acc + wb_ref[h] * jnp.maximum(s, 0.0)
        bits = pltpu.bitcast(acc, jnp.int32)
        key = bits ^ ((bits >> 31) & 0x7FFFFFFF)
        skey_ref[j] = jnp.where(col + j * ATT_TK <= row, key, I32_MIN)
        return carry

    lax.fori_loop(0, nt, score_tile, 0)

    half = ATT_TK // 2

    def bit_body(i, ans):
        cand_u = ans | lax.shift_left(jnp.int32(1), 31 - i)
        cand = jnp.broadcast_to(cand_u ^ I32_MIN, (IDX_QB, half))

        def count(j, c):
            t = skey_ref[j]
            return (c + jnp.where(t[:, :half] >= cand, 1, 0) + jnp.where(t[:, half:] >= cand, 1, 0))

        c = lax.fori_loop(0, nt, count, jnp.zeros((IDX_QB, half), jnp.int32))
        cnt = jnp.sum(c, axis=1, keepdims=True)
        return jnp.where(cnt >= TOPK_MAX, cand_u, ans)

    ans = lax.fori_loop(0, 32, bit_body, jnp.zeros((IDX_QB, 1), jnp.int32))
    thr = jnp.broadcast_to(jnp.maximum(ans ^ I32_MIN, I32_MIN + 1), (IDX_QB, ATT_TK))

    def write_tile(j, carry):
        mask_ref[0, j] = jnp.where(skey_ref[j] >= thr, 0.0, NEG).astype(BF16)
        return carry

    def write_empty(j, carry):
        mask_ref[0, j] = jnp.full((IDX_QB, ATT_TK), NEG, BF16)
        return carry

    lax.fori_loop(0, nt, write_tile, 0)
    lax.fori_loop(nt, n_tiles, write_empty, 0)


def _index_select(qi, ki, wi):
    s = qi.shape[0]
    n_tiles = s // ATT_TK
    return pl.pallas_call(
        _index_select_kernel,
        grid=(s // IDX_QB,),
        in_specs=[pl.BlockSpec((IDX_QB, IDX_HEADS * IDX_DIM), lambda b: (b, 0)),
                  pl.BlockSpec((IDX_QB, IDX_HEADS), lambda b: (b, 0)),
                  pl.BlockSpec((s, IDX_DIM), lambda b: (0, 0))],
        out_specs=pl.BlockSpec((1, n_tiles, IDX_QB, ATT_TK), lambda b: (b, 0, 0, 0)),
        out_shape=jax.ShapeDtypeStruct((s // IDX_QB, n_tiles, IDX_QB, ATT_TK), BF16),
        scratch_shapes=[pltpu.VMEM((n_tiles, IDX_QB, ATT_TK), jnp.int32),
                        pltpu.VMEM((IDX_HEADS, IDX_QB, ATT_TK), F32),
                        pltpu.VMEM((IDX_HEADS, IDX_QB, IDX_DIM), BF16)],
        compiler_params=pltpu.CompilerParams(dimension_semantics=("arbitrary",),
                                             vmem_limit_bytes=VMEM_LIMIT),
    )(qi, wi, ki.astype(BF16))


def _masked_attn_kernel(q_ref, k_ref, v_ref, mask_ref, bd_ref, bo_ref, o_ref,
                        qs_ref, m_ref, l_ref, acc_ref):
    grp = N_HEADS // N_KV
    qb = pl.program_id(1)
    q = (q_ref[...] * HEAD_DIM ** -0.5).astype(BF16)
    for hh in range(grp):
        qs_ref[hh] = q[:, hh * HEAD_DIM:(hh + 1) * HEAD_DIM]
    m_ref[...] = jnp.full(m_ref.shape, NEG, F32)
    l_ref[...] = jnp.zeros(l_ref.shape, F32)
    acc_ref[...] = jnp.zeros(acc_ref.shape, F32)

    def tile(j, bias_ref):
        off = pl.multiple_of(j * ATT_TK, ATT_TK)
        kt = k_ref[pl.ds(off, ATT_TK), :]
        vt = v_ref[pl.ds(off, ATT_TK), :]
        for r in range(ATT_QB // IDX_QB):
            rows = slice(r * IDX_QB, (r + 1) * IDX_QB)
            mk = mask_ref[r, j].astype(F32)
            for hh in range(grp):
                s = lax.dot_general(qs_ref[hh, rows], kt, _NT, preferred_element_type=F32) + mk
                if bias_ref is not None:
                    s = s + bias_ref[hh, rows]
                m_old = m_ref[hh, rows]
                m_new = jnp.maximum(m_old, jnp.max(s, axis=1, keepdims=True))
                alpha = jnp.exp(m_old - m_new)
                p = jnp.exp(s - jnp.concatenate([m_new] * (ATT_TK // HEAD_DIM), axis=1))
                l_ref[hh, rows] = alpha * l_ref[hh, rows] + jnp.sum(p, axis=1, keepdims=True)
                acc_ref[hh, rows] = alpha * acc_ref[hh, rows] + jnp.dot(
                    p.astype(BF16), vt, preferred_element_type=F32)
                m_ref[hh, rows] = m_new

    def far_tile(j, carry):
        tile(j, None)
        return carry

    lax.fori_loop(0, jnp.maximum(qb - 1, 0), far_tile, 0)

    @pl.when(qb >= 1)
    def _():
        tile(qb - 1, bo_ref)

    tile(qb, bd_ref)
    for hh in range(grp):
        o_ref[:, hh * HEAD_DIM:(hh + 1) * HEAD_DIM] = (acc_ref[hh] / l_ref[hh]).astype(o_ref.dtype)


def _rel_bias_tiles(rel_bias):
    a = jnp.arange(ATT_QB, dtype=jnp.int32)[:, None]
    c = jnp.arange(ATT_TK, dtype=jnp.int32)[None, :]
    diag = _rel_bias_of(a - c, rel_bias)
    prev = _rel_bias_of(a - c + ATT_TK, rel_bias)
    return diag.transpose(2, 0, 1), prev.transpose(2, 0, 1)


def _masked_attn(q, k, v, mask, rel_bias):
    s = q.shape[0]
    grp = N_HEADS // N_KV
    n_tiles = s // ATT_TK
    bd, bo = _rel_bias_tiles(rel_bias)
    wq = grp * HEAD_DIM
    return pl.pallas_call(
        _masked_attn_kernel,
        grid=(N_KV, s // ATT_QB),
        in_specs=[pl.BlockSpec((ATT_QB, wq), lambda g, i: (i, g)),
                  pl.BlockSpec((s, HEAD_DIM), lambda g, i: (0, g)),
                  pl.BlockSpec((s, HEAD_DIM), lambda g, i: (0, g)),
                  pl.BlockSpec((ATT_QB // IDX_QB, n_tiles, IDX_QB, ATT_TK), lambda g, i: (i, 0, 0, 0)),
                  pl.BlockSpec((grp, ATT_QB, ATT_TK), lambda g, i: (g, 0, 0)),
                  pl.BlockSpec((grp, ATT_QB, ATT_TK), lambda g, i: (g, 0, 0))],
        out_specs=pl.BlockSpec((ATT_QB, wq), lambda g, i: (i, g)),
        out_shape=jax.ShapeDtypeStruct((s, N_HEADS * HEAD_DIM), BF16),
        scratch_shapes=[pltpu.VMEM((grp, ATT_QB, HEAD_DIM), BF16),
                        pltpu.VMEM((grp, ATT_QB, HEAD_DIM), F32),
                        pltpu.VMEM((grp, ATT_QB, HEAD_DIM), F32),
                        pltpu.VMEM((grp, ATT_QB, HEAD_DIM), F32)],
        compiler_params=pltpu.CompilerParams(dimension_semantics=("arbitrary", "arbitrary"),
                                             vmem_limit_bytes=VMEM_LIMIT),
    )(q, k.astype(BF16), v.astype(BF16), mask, bd, bo)


SMP_PG = 8
SMP_TP = 8


def _order_key(score):
    bits = pltpu.bitcast(score, jnp.int32)
    return bits ^ ((bits >> 31) & 0x7FFFFFFF)


def _sample_select_kernel(pt_ref, *refs):
    pages = refs[:SMP_PG]
    kn_ref, qi_ref, w_ref, rep_ref, skey_ref, thr_ref = refs[SMP_PG:]
    p = pl.program_id(1)
    n_tiles = skey_ref.shape[1]
    wide = PAGE_SIZE * N_KV
    qi = qi_ref[0]
    w = jnp.concatenate([w_ref[0].reshape(IDX_HEADS, SMP_TP, PAGE_SIZE)] * N_KV, axis=2)

    def score(kt):
        kx = jnp.dot(rep_ref[...], kt.astype(BF16), preferred_element_type=F32).astype(BF16)
        s = lax.dot_general(qi, kx, _NT, preferred_element_type=F32)
        s = jnp.maximum(s, 0.0).reshape(IDX_HEADS, SMP_TP, wide)
        return _order_key(jnp.sum(w * s, axis=0))

    for i in range(SMP_PG):
        skey_ref[0, p * SMP_PG + i] = score(pages[i][0, 0])

    @pl.when(p == pl.num_programs(1) - 1)
    def _():
        t = lax.broadcasted_iota(jnp.int32, (SMP_TP, wide), 0)
        j = lax.broadcasted_iota(jnp.int32, (SMP_TP, wide), 1) // N_KV
        skey_ref[0, n_tiles - 1] = jnp.where((j <= t) & (j < DEC_SEQ), score(kn_ref[0]), I32_MIN)

        def bit_body(i, ans):
            cand_u = ans | lax.shift_left(jnp.int32(1), 31 - i)
            cand = jnp.broadcast_to(cand_u ^ I32_MIN, (SMP_TP, wide))

            def count(jt, c):
                return c + jnp.where(skey_ref[0, jt] >= cand, 1, 0)

            c = lax.fori_loop(0, n_tiles, count, jnp.zeros((SMP_TP, wide), jnp.int32))
            return jnp.where(jnp.sum(c, axis=1, keepdims=True) >= TOPK_MAX * N_KV, cand_u, ans)

        ans = lax.fori_loop(0, 32, bit_body, jnp.zeros((SMP_TP, 1), jnp.int32))
        thr_ref[0] = jnp.broadcast_to(jnp.maximum(ans ^ I32_MIN, I32_MIN + 1), (SMP_TP, PAGE_SIZE))


def _sample_attn_kernel(pt_ref, *refs):
    kp, vp = refs[:SMP_PG], refs[SMP_PG:2 * SMP_PG]
    kn_ref, vn_ref, q_ref, skey_ref, thr_ref, hm_ref, hl_ref, hn_ref, o_ref, m_ref, l_ref, acc_ref = (
        refs[2 * SMP_PG:])
    p = pl.program_id(1)
    last = p == pl.num_programs(1) - 1
    n_tiles = skey_ref.shape[1]
    wide = PAGE_SIZE * N_KV

    @pl.when(p == 0)
    def _():
        m_ref[...] = jnp.full(m_ref.shape, NEG, F32)
        l_ref[...] = jnp.zeros(l_ref.shape, F32)
        acc_ref[...] = jnp.zeros(acc_ref.shape, F32)

    q = q_ref[0]
    thr = jnp.concatenate([thr_ref[0]] * N_KV, axis=1)

    def update(tile, k_rows, v_rows, head_mask):
        sel = jnp.where(skey_ref[0, tile] >= thr, 0.0, NEG)
        s = lax.dot_general(q, k_rows.astype(BF16), _NT, preferred_element_type=F32)
        s = s + jnp.concatenate([sel] * N_HEADS, axis=0) + head_mask
        m_old = m_ref[...]
        m_new = jnp.maximum(m_old, jnp.max(s, axis=1, keepdims=True))
        alpha = jnp.exp(m_old - m_new)
        pr = jnp.exp(s - jnp.concatenate([m_new] * (wide // HEAD_DIM), axis=1))
        l_ref[...] = alpha * l_ref[...] + jnp.sum(pr, axis=1, keepdims=True)
        acc_ref[...] = alpha * acc_ref[...] + jnp.dot(pr.astype(BF16), v_rows.astype(BF16),
                                                      preferred_element_type=F32)
        m_ref[...] = m_new

    for i in range(SMP_PG):
        head_mask = jnp.where(last, hl_ref[...], hm_ref[...]) if i == SMP_PG - 1 else hm_ref[...]
        update(p * SMP_PG + i, kp[i][0, 0].reshape(wide, HEAD_DIM), vp[i][0, 0].reshape(wide, HEAD_DIM),
               head_mask)

    @pl.when(last)
    def _():
        update(n_tiles - 1, kn_ref[0], vn_ref[0], hn_ref[...])
        o_ref[0] = acc_ref[...] / l_ref[...]


def _sample_attn_pallas(q, k, v, qi, ki, wi, cache_k, cache_v, cache_idx_k, page_table, rel_bias):
    bsz, nq = q.shape[:2]
    n_pages = page_table.shape[1]
    past = n_pages * PAGE_SIZE
    grp = N_HEADS // N_KV
    n_steps = n_pages // SMP_PG
    tpad = ((0, 0), (0, 0), (0, SMP_TP - nq), (0, 0))
    kpad = ((0, 0), (0, PAGE_SIZE - nq), (0, 0))
    qi_r = jnp.pad((qi * IDX_DIM ** -0.5).reshape(bsz, nq, IDX_HEADS, IDX_DIM).transpose(0, 2, 1, 3), tpad)
    qi_r = qi_r.reshape(bsz, IDX_HEADS * SMP_TP, IDX_DIM).astype(BF16)
    w_r = jnp.pad((wi * IDX_HEADS ** -0.5).transpose(0, 2, 1), ((0, 0), (0, 0), (0, SMP_TP - nq)))
    w_r = jnp.broadcast_to(w_r.reshape(bsz, IDX_HEADS * SMP_TP, 1), (bsz, IDX_HEADS * SMP_TP, PAGE_SIZE))
    page_specs = lambda *tail: [
        pl.BlockSpec((1, 1, PAGE_SIZE) + tail, functools.partial(
            lambda b, p, pt, i: (0, pt[b, p * SMP_PG + i], 0) + (0,) * len(tail), i=i))
        for i in range(SMP_PG)]
    per_batch = lambda *shape: pl.BlockSpec((1,) + shape, lambda b, p, pt: (b,) + (0,) * len(shape))
    const = lambda *shape: pl.BlockSpec(shape, lambda b, p, pt: (0,) * len(shape))
    wide = PAGE_SIZE * N_KV
    col_pos = jnp.arange(wide, dtype=jnp.int32) // N_KV
    col_kv = jnp.arange(wide, dtype=jnp.int32) % N_KV
    rep = (col_pos[:, None] == jnp.arange(PAGE_SIZE, dtype=jnp.int32)[None, :]).astype(BF16)
    skey, thr = pl.pallas_call(
        _sample_select_kernel,
        grid_spec=pltpu.PrefetchScalarGridSpec(
            num_scalar_prefetch=1,
            grid=(bsz, n_steps),
            in_specs=page_specs(IDX_DIM) + [per_batch(PAGE_SIZE, IDX_DIM),
                                            per_batch(IDX_HEADS * SMP_TP, IDX_DIM),
                                            per_batch(IDX_HEADS * SMP_TP, PAGE_SIZE),
                                            const(wide, PAGE_SIZE)],
            out_specs=[per_batch(n_pages + 1, SMP_TP, wide), per_batch(SMP_TP, PAGE_SIZE)]),
        out_shape=[jax.ShapeDtypeStruct((bsz, n_pages + 1, SMP_TP, wide), jnp.int32),
                   jax.ShapeDtypeStruct((bsz, SMP_TP, PAGE_SIZE), jnp.int32)],
        compiler_params=pltpu.CompilerParams(dimension_semantics=("arbitrary", "arbitrary"),
                                             vmem_limit_bytes=VMEM_LIMIT),
    )(page_table, *([cache_idx_k] * SMP_PG), jnp.pad(ki, kpad), qi_r, w_r, rep)

    n_rows = N_HEADS * SMP_TP
    q_r = jnp.pad((q * HEAD_DIM ** -0.5).reshape(bsz, nq, N_HEADS, HEAD_DIM).transpose(0, 2, 1, 3), tpad)
    q_r = q_r.reshape(bsz, n_rows, HEAD_DIM).astype(BF16)
    row_kv = jnp.arange(n_rows, dtype=jnp.int32) // (grp * SMP_TP)
    head_mask = jnp.where(row_kv[:, None] == col_kv[None, :], 0.0, NEG).astype(F32)
    t = jnp.arange(SMP_TP, dtype=jnp.int32)[:, None]
    pos = jnp.arange(PAGE_SIZE, dtype=jnp.int32)[None, :]
    bias_of = lambda dist: jnp.repeat(
        _rel_bias_of(dist, rel_bias).transpose(2, 0, 1).reshape(n_rows, PAGE_SIZE), N_KV, axis=1)
    mask_last = head_mask + bias_of(PAGE_SIZE + t - pos)
    mask_new = head_mask + bias_of(t - pos)
    new_rows = lambda a: jnp.pad(a.reshape(bsz, nq * N_KV, HEAD_DIM), ((0, 0), (0, wide - nq * N_KV), (0, 0)))
    o = pl.pallas_call(
        _sample_attn_kernel,
        grid_spec=pltpu.PrefetchScalarGridSpec(
            num_scalar_prefetch=1,
            grid=(bsz, n_steps),
            in_specs=page_specs(N_KV, HEAD_DIM) + page_specs(N_KV, HEAD_DIM) + [
                per_batch(wide, HEAD_DIM), per_batch(wide, HEAD_DIM),
                per_batch(n_rows, HEAD_DIM),
                per_batch(n_pages + 1, SMP_TP, wide), per_batch(SMP_TP, PAGE_SIZE),
                const(n_rows, wide), const(n_rows, wide), const(n_rows, wide)],
            out_specs=per_batch(n_rows, HEAD_DIM),
            scratch_shapes=[pltpu.VMEM((n_rows, HEAD_DIM), F32)] * 3),
        out_shape=jax.ShapeDtypeStruct((bsz, n_rows, HEAD_DIM), F32),
        compiler_params=pltpu.CompilerParams(dimension_semantics=("arbitrary", "arbitrary"),
                                             vmem_limit_bytes=VMEM_LIMIT),
    )(page_table, *([cache_k] * SMP_PG), *([cache_v] * SMP_PG),
      new_rows(k), new_rows(v), q_r, skey, thr, head_mask, mask_last, mask_new)
    o = o.reshape(bsz, N_HEADS, SMP_TP, HEAD_DIM)[:, :, :nq]
    return o.transpose(0, 2, 1, 3).reshape(bsz, nq, N_HEADS * HEAD_DIM)


LRU_TT = 512
_GELU_C = math.sqrt(2.0 / math.pi)


def _gelu_tanh(y):
    return 0.5 * y * (1.0 + jnp.tanh(_GELU_C * (y + 0.044715 * (y * y * y))))


def _lru_coeffs(xc, wa_ref, ba_ref, wi_ref, bi_ref, lam_ref):
    xb = xc.astype(BF16)
    r = jax.nn.sigmoid(jnp.dot(xb, wa_ref[0].astype(BF16), preferred_element_type=F32) + ba_ref[0])
    i = jax.nn.sigmoid(jnp.dot(xb, wi_ref[0].astype(BF16), preferred_element_type=F32) + bi_ref[0])
    z = -lam_ref[...]
    softplus = jnp.maximum(z, 0.0) + jnp.log1p(jnp.exp(-jnp.abs(z)))
    log_a = -LRU_C * r * softplus
    t = jnp.tanh(log_a)
    u = jnp.sqrt(-2.0 * t / (1.0 - t)) * (i * xc)
    return jnp.exp(log_a), u


def _rglru_prompt_kernel(x_ref, y_ref, cw_ref, cb_ref, wa_ref, ba_ref, wi_ref, bi_ref, lam_ref,
                         o_ref, nb_ref, hl_ref, xbuf_ref, hc_ref):
    c = pl.program_id(1)
    tt = x_ref.shape[0]

    @pl.when(c == 0)
    def _():
        xbuf_ref[0:8] = jnp.zeros((8, RNN_BW), F32)
        hc_ref[...] = jnp.zeros(hc_ref.shape, F32)

    x = x_ref[...]
    xbuf_ref[8:8 + tt] = x
    xc = cb_ref[...] + cw_ref[CONV_W - 1:CONV_W] * x
    for j in range(CONV_W - 1):
        xc = xc + cw_ref[j:j + 1] * xbuf_ref[8 - (CONV_W - 1) + j:8 - (CONV_W - 1) + j + tt]
    a, u = _lru_coeffs(xc, wa_ref, ba_ref, wi_ref, bi_ref, lam_ref)
    row = lax.broadcasted_iota(jnp.int32, (tt, RNN_BW), 0)
    s = 1
    while s < tt:
        keep = row >= s
        a_prev = jnp.where(keep, pltpu.roll(a, s, axis=0), 1.0)
        u_prev = jnp.where(keep, pltpu.roll(u, s, axis=0), 0.0)
        u = a * u_prev + u
        a = a * a_prev
        s *= 2
    h = a * hc_ref[0:1] + u
    o_ref[...] = (h * _gelu_tanh(y_ref[...])).astype(o_ref.dtype)
    hc_ref[0:1] = h[tt - 1:tt]
    xbuf_ref[0:8] = x[tt - 8:tt]

    @pl.when(c == pl.num_programs(1) - 1)
    def _():
        hl_ref[...] = h[tt - 1:tt]
        nb_ref[...] = x[tt - (CONV_W - 1):tt]


def _rglru_prompt(xr, yr, s, conv_w, conv_b, w_rg_a, b_rg_a, w_rg_i, b_rg_i, lru_lambda):
    vec = lambda t: t.reshape(1, D_RNN)
    blk = lambda t: t.reshape(RNN_BLOCKS, 1, RNN_BW)
    row_spec = pl.BlockSpec((LRU_TT, RNN_BW), lambda n, c: (c, n))
    vec_spec = pl.BlockSpec((1, RNN_BW), lambda n, c: (0, n))
    w_spec = pl.BlockSpec((1, RNN_BW, RNN_BW), lambda n, c: (n, 0, 0))
    b_spec = pl.BlockSpec((1, 1, RNN_BW), lambda n, c: (n, 0, 0))
    return pl.pallas_call(
        _rglru_prompt_kernel,
        grid=(RNN_BLOCKS, s // LRU_TT),
        in_specs=[row_spec, row_spec, pl.BlockSpec((CONV_W, RNN_BW), lambda n, c: (0, n)), vec_spec,
                  w_spec, b_spec, w_spec, b_spec, vec_spec],
        out_specs=[row_spec, pl.BlockSpec((CONV_W - 1, RNN_BW), lambda n, c: (0, n)), vec_spec],
        out_shape=[jax.ShapeDtypeStruct((s, D_RNN), BF16),
                   jax.ShapeDtypeStruct((CONV_W - 1, D_RNN), F32),
                   jax.ShapeDtypeStruct((1, D_RNN), F32)],
        scratch_shapes=[pltpu.VMEM((8 + LRU_TT, RNN_BW), F32), pltpu.VMEM((8, RNN_BW), F32)],
        compiler_params=pltpu.CompilerParams(dimension_semantics=("arbitrary", "arbitrary"),
                                             vmem_limit_bytes=VMEM_LIMIT),
    )(xr, yr, conv_w, vec(conv_b), w_rg_a, blk(b_rg_a), w_rg_i, blk(b_rg_i), vec(lru_lambda))


def _rglru_sample_kernel(x_ref, y_ref, buf_ref, h0_ref, cw_ref, cb_ref, wa_ref, ba_ref, wi_ref, bi_ref,
                         lam_ref, o_ref, nb_ref, hl_ref):
    nt, nb = x_ref.shape[0], x_ref.shape[1]
    xs = [buf_ref[j] for j in range(CONV_W - 1)] + [x_ref[t] for t in range(nt)]
    xc = []
    for t in range(nt):
        acc = cb_ref[...] + cw_ref[0:1] * xs[t]
        for j in range(1, CONV_W):
            acc = acc + cw_ref[j:j + 1] * xs[t + j]
        xc.append(acc)
    a, u = _lru_coeffs(jnp.concatenate(xc, axis=0), wa_ref, ba_ref, wi_ref, bi_ref, lam_ref)
    h = h0_ref[...]
    for t in range(nt):
        h = a[t * nb:(t + 1) * nb] * h + u[t * nb:(t + 1) * nb]
        o_ref[t] = (h * _gelu_tanh(y_ref[t])).astype(o_ref.dtype)
    hl_ref[...] = h
    for j in range(CONV_W - 1):
        nb_ref[j] = xs[nt + j]


def _rglru_sample(xr, yr, buf, h0, conv_w, conv_b, w_rg_a, b_rg_a, w_rg_i, b_rg_i, lru_lambda):
    nt, nb, _ = xr.shape
    vec = lambda t: t.reshape(1, D_RNN)
    blk = lambda t: t.reshape(RNN_BLOCKS, 1, RNN_BW)
    seq_spec = lambda n_rows: pl.BlockSpec((n_rows, nb, RNN_BW), lambda n: (0, 0, n))
    vec_spec = pl.BlockSpec((1, RNN_BW), lambda n: (0, n))
    w_spec = pl.BlockSpec((1, RNN_BW, RNN_BW), lambda n: (n, 0, 0))
    b_spec = pl.BlockSpec((1, 1, RNN_BW), lambda n: (n, 0, 0))
    h_spec = pl.BlockSpec((nb, RNN_BW), lambda n: (0, n))
    return pl.pallas_call(
        _rglru_sample_kernel,
        grid=(RNN_BLOCKS,),
        in_specs=[seq_spec(nt), seq_spec(nt), seq_spec(CONV_W - 1), h_spec,
                  pl.BlockSpec((CONV_W, RNN_BW), lambda n: (0, n)), vec_spec,
                  w_spec, b_spec, w_spec, b_spec, vec_spec],
        out_specs=[seq_spec(nt), seq_spec(CONV_W - 1), h_spec],
        out_shape=[jax.ShapeDtypeStruct((nt, nb, D_RNN), BF16),
                   jax.ShapeDtypeStruct((CONV_W - 1, nb, D_RNN), F32),
                   jax.ShapeDtypeStruct((nb, D_RNN), F32)],
        compiler_params=pltpu.CompilerParams(dimension_semantics=("arbitrary",),
                                             vmem_limit_bytes=VMEM_LIMIT),
    )(xr, yr, buf, h0, conv_w, vec(conv_b), w_rg_a, blk(b_rg_a), w_rg_i, blk(b_rg_i), vec(lru_lambda))


def _layernorm(x, g, b):
    mu = jnp.mean(x, axis=-1, keepdims=True)
    var = jnp.mean(jnp.square(x - mu), axis=-1, keepdims=True)
    return (x - mu) * lax.rsqrt(var + LN_EPS) * g + b


def _rel_bucket(dist):
    max_exact = N_BUCKETS // 2
    d = jnp.maximum(dist, 0)
    large = max_exact + (jnp.log(jnp.maximum(d, 1).astype(F32) / max_exact)
                         / math.log(MAX_DIST / max_exact) * (N_BUCKETS - max_exact)).astype(jnp.int32)
    large = jnp.minimum(large, N_BUCKETS - 1)
    return jnp.where(d < max_exact, d, large)


def _rel_bias_of(dist, rel_bias):
    onehot = jax.nn.one_hot(_rel_bucket(dist), N_BUCKETS, dtype=F32)
    return jnp.dot(onehot, rel_bias - rel_bias[N_BUCKETS - 1], precision=lax.Precision.HIGHEST)


def _dispatch(eidx, hit):
    n = eidx.shape[0]
    incl = jnp.cumsum(hit, axis=0)
    counts = incl[-1]
    pcounts = (counts + MOE_MB - 1) // MOE_MB * MOE_MB
    pend = jnp.cumsum(pcounts)
    pstart = pend - pcounts
    dest = jnp.take_along_axis(incl - hit + pstart[None, :], eidx, axis=1).astype(jnp.int32)
    flat = dest.reshape(-1)
    slot_tok = jnp.full((MOE_NB * MOE_MB,), n, jnp.int32).at[flat].set(
        jnp.arange(n * TOP_K, dtype=jnp.int32) // TOP_K)
    block_start = jnp.arange(MOE_NB, dtype=jnp.int32) * MOE_MB
    blk_e = jnp.minimum(jnp.sum((pend[None, :] <= block_start[:, None]).astype(jnp.int32), axis=1),
                        N_EXPERTS - 1)
    n_used = (pend[-1:] // MOE_MB).astype(jnp.int32)
    return slot_tok, dest, blk_e, n_used


def kernel(x_prompt, x_sample, cache_k, cache_v, cache_idx_k, state_conv, state_rnn, page_table,
           p_prompt, p_sample, rel_bias, w_in, conv_w, conv_b, w_rg_a, b_rg_a, w_rg_i, b_rg_i,
           lru_lambda, w_rnn_out, w_att_out, w_out, ln1_g, ln1_b, w_router, router_bias,
           w_e1, w_e3, w_e2, w_s1, w_s3, w_s2, ln2_g, ln2_b, w_ple_gate, w_ple_proj):
    ns = DEC_BATCH * DEC_SEQ
    x_all = jnp.concatenate([x_prompt[0], x_sample.reshape(ns, D_MODEL)], axis=0)
    xb = x_all.astype(BF16)
    win = w_in[0]
    offs = [0]
    for w in SPLITS:
        offs.append(offs[-1] + w)
    xr, yr, q, k, v, qi = [_mm(xb, win, offs[g], SPLITS[g]) for g in range(6)]
    w_small = jnp.pad(win[:, offs[6]:offs[8]], ((0, 0), (0, 128 - IDX_DIM - IDX_HEADS)))
    kw = _mm(xb, w_small, tn=128)
    ki, wi = kw[:, :IDX_DIM], kw[:, IDX_DIM:IDX_DIM + IDX_HEADS]
    w_gate = win[:, offs[8]:]
    g_rnn = _mm(xb, w_gate, 0, D_MODEL)
    g_att = _mm(xb, w_gate, D_MODEL, D_MODEL)

    lw = (conv_w[0], conv_b[0], w_rg_a[0], b_rg_a[0], w_rg_i[0], b_rg_i[0], lru_lambda[0])
    smp = lambda t: t[SEQ:].reshape(DEC_BATCH, DEC_SEQ, -1)
    tmaj = lambda t: t.transpose(1, 0, 2)
    rnn_p, buf_p, h_p = _rglru_prompt(xr, yr, SEQ, *lw)
    rnn_s, buf_s, h_s = _rglru_sample(tmaj(smp(xr)), tmaj(smp(yr)), tmaj(state_conv[0]), state_rnn[0], *lw)
    buf_s = tmaj(buf_s)

    k_p, v_p, ki_p = k[:SEQ], v[:SEQ], ki[:SEQ]
    k_s, v_s, ki_s = smp(k), smp(v), smp(ki)
    att_p = _masked_attn(q[:SEQ], k_p, v_p, _index_select(qi[:SEQ], ki_p, wi[:SEQ]), rel_bias)
    att_s = _sample_attn_pallas(smp(q), k_s, v_s, smp(qi), ki_s, smp(wi),
                                cache_k, cache_v, cache_idx_k, page_table, rel_bias)
    hd = lambda t: t.reshape(t.shape[:-1] + (N_KV, HEAD_DIM))
    k_p, v_p, k_s, v_s = hd(k_p)[None], hd(v_p)[None], hd(k_s), hd(v_s)
    ki_p, buf_p = ki_p[None], buf_p[None]

    rnn_o = jnp.concatenate([rnn_p, tmaj(rnn_s).reshape(ns, D_RNN)], axis=0)
    att_o = jnp.concatenate([att_p, att_s.reshape(ns, D_MODEL).astype(BF16)], axis=0)
    merged = (jax.nn.sigmoid(g_rnn) * _mm(rnn_o, w_rnn_out[0])
              + jax.nn.sigmoid(g_att) * _mm(att_o, w_att_out[0]))
    x1 = _layernorm(DN_ALPHA * x_all + _mm(merged.astype(BF16), w_out[0]), ln1_g[0], ln1_b[0])

    eidx, gw, hit = _router(x1, w_router[0], router_bias[0])
    slot_tok, dest, blk_e, n_used = _dispatch(eidx, hit)
    x1b = x1.astype(BF16)
    xpad = jnp.concatenate([x1b, jnp.zeros((1, D_MODEL), BF16)], axis=0)
    yb = _experts(xpad[slot_tok], blk_e, n_used, w_e1[0], w_e3[0], w_e2[0])
    routed = jnp.sum(yb[dest].astype(F32) * gw[:, :, None], axis=1)
    hs = _mm(x1b, w_s1[0])
    hs = (hs * jax.nn.sigmoid(hs) * _mm(x1b, w_s3[0])).astype(BF16)
    moe = routed + _mm(hs, w_s2[0])
    x2 = _layernorm(DN_ALPHA * x1 + moe, ln2_g[0], ln2_b[0])

    p_all = jnp.concatenate([p_prompt[0, 0], p_sample[0].reshape(ns, -1)], axis=0).astype(BF16)
    y = x2 + jax.nn.sigmoid(_mm(x2.astype(BF16), w_ple_gate[0])) * _mm(p_all, w_ple_proj[0])

    return (y[:SEQ][None], y[SEQ:].reshape(DEC_BATCH, DEC_SEQ, D_MODEL),
            k_p[None], v_p[None], ki_p[None], buf_p[None], h_p[None],
            k_s[None], v_s[None], ki_s[None], buf_s[None], h_s[None])
```
